```python
import math
import jax, jax.numpy as jnp
from jax import lax
import numpy as np

D_MODEL = 1024
BATCH = 1
SEQ = 16384
DEPTH = 4

CHUNK = 64
Q_BLOCK = 128
N_MIXERS = 3
N_A = (DEPTH + 2) // 3
N_B = (DEPTH + 1) // 3
N_C = DEPTH // 3
DEEPNORM_ALPHA = (2.0 * DEPTH) ** 0.25
DEEPNORM_BETA = (8.0 * DEPTH) ** -0.25
LN_EPS = 1e-5
RMS_EPS = 1e-6
ROPE_BASE = 10000.0

LRU_WIDTH = D_MODEL
LRU_BLOCKS = 4
LRU_BLOCK_W = LRU_WIDTH // LRU_BLOCKS
LRU_CONV_W = 4
LRU_C = 8.0

MLA_HEADS = 16
QK_NOPE = 64
QK_ROPE = 32
V_HEAD = 64
KV_LORA = 256
Q_LORA = 768

RET_HEADS = 4
RET_QK = D_MODEL // RET_HEADS
RET_V = 2 * D_MODEL // RET_HEADS

D_FF = 2816
FFN_CONV_W = 3

kernel_name = "hybrid_rglru_mla_retention_deepnorm"


def layer_norm(x, g, b):
    xf = x.astype(jnp.float32)
    mu = jnp.mean(xf, axis=-1, keepdims=True)
    var = jnp.mean(jnp.square(xf - mu), axis=-1, keepdims=True)
    return (xf - mu) * lax.rsqrt(var + LN_EPS) * g + b


def rms_norm(x, g):
    xf = x.astype(jnp.float32)
    return (xf * lax.rsqrt(jnp.mean(xf * xf, axis=-1, keepdims=True) + RMS_EPS) * g).astype(x.dtype)


def rope(x, positions):
    d = x.shape[-1]
    inv_freq = ROPE_BASE ** (-jnp.arange(0, d, 2, dtype=jnp.float32) / d)
    ang = positions.astype(jnp.float32)[:, :, None, None] * inv_freq
    cos, sin = jnp.cos(ang), jnp.sin(ang)
    xf = x.astype(jnp.float32)
    x1, x2 = xf[..., : d // 2], xf[..., d // 2:]
    return jnp.concatenate([x1 * cos - x2 * sin, x1 * sin + x2 * cos], axis=-1)


def causal_dwconv(x, w, b):
    k_w, c = w.shape
    y = lax.conv_general_dilated(
        x, w.astype(x.dtype)[:, None, :], window_strides=(1,), padding=[(k_w - 1, 0)],
        dimension_numbers=("NWC", "WIO", "NWC"), feature_group_count=c)
    return y + b.astype(x.dtype)


def rglru_mixer(x, w_in, conv_w, conv_b, w_a, b_a, w_x, b_x, lam, w_out):
    bsz, s, _ = x.shape
    proj = x @ w_in
    gate_branch, rnn_in = jnp.split(proj, 2, axis=-1)
    u = causal_dwconv(rnn_in, conv_w, conv_b)
    ub = u.reshape(bsz, s, LRU_BLOCKS, LRU_BLOCK_W)
    r = jax.nn.sigmoid(jnp.einsum("bsgi,gij->bsgj", ub, w_a) + b_a).reshape(bsz, s, LRU_WIDTH)
    i = jax.nn.sigmoid(jnp.einsum("bsgi,gij->bsgj", ub, w_x) + b_x).reshape(bsz, s, LRU_WIDTH)
    log_a = -LRU_C * r.astype(jnp.float32) * jax.nn.softplus(-lam.astype(jnp.float32))
    a = jnp.exp(log_a)
    mult = jnp.sqrt(-jnp.expm1(2.0 * log_a))
    bvals = mult * (i * u).astype(jnp.float32)

    def combine(c1, c2):
        a1, b1 = c1
        a2, b2 = c2
        return a1 * a2, a2 * b1 + b2

    _, h = lax.associative_scan(combine, (a, bvals), axis=1)
    y = jax.nn.gelu(gate_branch) * h.astype(x.dtype)
    return y @ w_out


def chunk_causal_attention(q_nope, q_pe, k_nope, k_pe, v):
    s_len = q_nope.shape[1]
    scale = (QK_NOPE + QK_ROPE) ** -0.5
    outs = []
    for j in range(s_len // Q_BLOCK):
        q0 = j * Q_BLOCK
        kv_len = q0 + Q_BLOCK
        sc = (jnp.einsum("bqhd,bkhd->bhqk", q_nope[:, q0:kv_len], k_nope[:, :kv_len])
              + jnp.einsum("bqhd,bkd->bhqk", q_pe[:, q0:kv_len], k_pe[:, :kv_len]))
        sc = sc.astype(jnp.float32) * scale
        q_chunk = (q0 + jnp.arange(Q_BLOCK)) // CHUNK
        k_chunk = jnp.arange(kv_len) // CHUNK
        mask = k_chunk[None, :] <= q_chunk[:, None]
        sc = jnp.where(mask, sc, -jnp.inf)
        p = jax.nn.softmax(sc, axis=-1).astype(v.dtype)
        outs.append(jnp.einsum("bhqk,bkhd->bqhd", p, v[:, :kv_len]))
    return jnp.concatenate(outs, axis=1)


def mla_mixer(x, positions, w_in, q_norm_g, kv_norm_g, w_uq, w_ukv, w_out):
    bsz, s, _ = x.shape
    proj = x @ w_in
    c_q = proj[..., :Q_LORA]
    c_kv = proj[..., Q_LORA:Q_LORA + KV_LORA]
    k_pe = proj[..., Q_LORA + KV_LORA:]
    q = (rms_norm(c_q, q_norm_g) @ w_uq).reshape(bsz, s, MLA_HEADS, QK_NOPE + QK_ROPE)
    q_nope, q_pe = q[..., :QK_NOPE], q[..., QK_NOPE:]
    q_pe = rope(q_pe, positions).astype(x.dtype)
    k_pe = rope(k_pe[:, :, None, :], positions)[:, :, 0, :].astype(x.dtype)
    kv = (rms_norm(c_kv, kv_norm_g) @ w_ukv).reshape(bsz, s, MLA_HEADS, QK_NOPE + V_HEAD)
    k_nope, v = kv[..., :QK_NOPE], kv[..., QK_NOPE:]
    o = chunk_causal_attention(q_nope, q_pe, k_nope, k_pe, v)
    return o.reshape(bsz, s, MLA_HEADS * V_HEAD) @ w_out


def retention_mixer(x, positions, w_in, gn_g, w_out):
    bsz, s, _ = x.shape
    n_c = s // CHUNK
    hq, hv = RET_HEADS * RET_QK, RET_HEADS * RET_V
    proj = x @ w_in
    q = rope(proj[..., :hq].reshape(bsz, s, RET_HEADS, RET_QK), positions)
    k = rope(proj[..., hq:2 * hq].reshape(bsz, s, RET_HEADS, RET_QK), positions) * (RET_QK ** -0.5)
    v = proj[..., 2 * hq:2 * hq + hv].reshape(bsz, s, RET_HEADS, RET_V).astype(jnp.float32)
    g = proj[..., 2 * hq + hv:]
    qc = q.reshape(bsz, n_c, CHUNK, RET_HEADS, RET_QK)
    kc = k.reshape(bsz, n_c, CHUNK, RET_HEADS, RET_QK)
    vc = v.reshape(bsz, n_c, CHUNK, RET_HEADS, RET_V)
    log_gamma = jnp.log1p(-jnp.exp2(-5.0 - jnp.arange(RET_HEADS, dtype=jnp.float32)))
    idx = jnp.arange(CHUNK, dtype=jnp.float32)
    rel = jnp.abs(idx[:, None] - idx[None, :])
    d_intra = jnp.exp(log_gamma[:, None, None] * rel)
    sc = jnp.einsum("bnqhd,bnkhd->bnhqk", qc, kc) * d_intra
    y_intra = jnp.einsum("bnhqk,bnkhe->bnqhe", sc, vc)
    xi = jnp.exp(log_gamma[None, :] * (idx + 1.0)[:, None])
    zeta = jnp.exp(log_gamma[None, :] * (CHUNK - 1.0 - idx)[:, None])
    chunk_decay = jnp.exp(log_gamma * CHUNK)

    def step(state, inp):
        qn, kn, vn = inp
        y = jnp.einsum("bqhd,bhde->bqhe", qn, state) * xi[None, :, :, None]
        state = state * chunk_decay[None, :, None, None] + jnp.einsum(
            "bkhd,bkhe->bhde", kn * zeta[None, :, :, None], vn)
        return state, y

    state0 = jnp.zeros((bsz, RET_HEADS, RET_QK, RET_V), jnp.float32)
    _, y_inter = lax.scan(step, state0, (jnp.swapaxes(qc, 0, 1), jnp.swapaxes(kc, 0, 1),
                                         jnp.swapaxes(vc, 0, 1)))
    y = (y_intra + jnp.swapaxes(y_inter, 0, 1)).reshape(bsz, s, RET_HEADS, RET_V)
    mu = jnp.mean(y, axis=-1, keepdims=True)
    var = jnp.mean(jnp.square(y - mu), axis=-1, keepdims=True)
    y = ((y - mu) * lax.rsqrt(var + LN_EPS)).reshape(bsz, s, hv) * gn_g
    return (jax.nn.silu(g.astype(jnp.float32)) * y).astype(x.dtype) @ w_out


def conv_gated_ffn(x, w_up, conv_w, conv_b, w_down):
    h = causal_dwconv(x @ w_up, conv_w, conv_b)
    gate, up = jnp.split(h, 2, axis=-1)
    return (jax.nn.gelu(gate) * up) @ w_down


def setup_inputs(seed: int = 0) -> dict:
    key = jax.random.key(seed)
    ks = iter(jax.random.split(key, 64))
    f32 = jnp.float32

    def nrm(shape, scale):
        return scale * jax.random.normal(next(ks), shape, f32)

    def gain(shape):
        return 1.0 + nrm(shape, 0.02)

    W, F, D = LRU_WIDTH, D_FF, D_MODEL
    x = jax.random.normal(next(ks), (BATCH, SEQ, D), f32)
    positions = jnp.broadcast_to(jnp.arange(SEQ, dtype=jnp.int32), (BATCH, SEQ))
    a0 = jax.random.uniform(next(ks), (N_A, W), f32, 0.9, 0.999)
    s0 = a0 ** (1.0 / LRU_C)
    return {
        "x": x,
        "positions": positions,
        "ln1_g": gain((DEPTH, D)), "ln1_b": nrm((DEPTH, D), 0.01),
        "ln2_g": gain((DEPTH, D)), "ln2_b": nrm((DEPTH, D), 0.01),
        "ffn_w_up": nrm((DEPTH, D, 2 * F), D ** -0.5),
        "ffn_conv_w": nrm((DEPTH, FFN_CONV_W, 2 * F), FFN_CONV_W ** -0.5),
        "ffn_conv_b": nrm((DEPTH, 2 * F), 0.01),
        "ffn_w_down": nrm((DEPTH, F, D), DEEPNORM_BETA * F ** -0.5),
        "lru_w_in": nrm((N_A, D, 2 * W), D ** -0.5),
        "lru_conv_w": nrm((N_A, LRU_CONV_W, W), LRU_CONV_W ** -0.5),
        "lru_conv_b": nrm((N_A, W), 0.01),
        "lru_w_a": nrm((N_A, LRU_BLOCKS, LRU_BLOCK_W, LRU_BLOCK_W), LRU_BLOCK_W ** -0.5),
        "lru_b_a": nrm((N_A, LRU_BLOCKS, LRU_BLOCK_W), 0.01),
        "lru_w_x": nrm((N_A, LRU_BLOCKS, LRU_BLOCK_W, LRU_BLOCK_W), LRU_BLOCK_W ** -0.5),
        "lru_b_x": nrm((N_A, LRU_BLOCKS, LRU_BLOCK_W), 0.01),
        "lru_lambda": jnp.log(s0) - jnp.log1p(-s0),
        "lru_w_out": nrm((N_A, W, D), DEEPNORM_BETA * W ** -0.5),
        "mla_w_in": nrm((N_B, D, Q_LORA + KV_LORA + QK_ROPE), D ** -0.5),
        "mla_q_norm": gain((N_B, Q_LORA)),
        "mla_kv_norm": gain((N_B, KV_LORA)),
        "mla_w_uq": nrm((N_B, Q_LORA, MLA_HEADS * (QK_NOPE + QK_ROPE)), Q_LORA ** -0.5),
        "mla_w_ukv": nrm((N_B, KV_LORA, MLA_HEADS * (QK_NOPE + V_HEAD)), KV_LORA ** -0.5),
        "mla_w_out": nrm((N_B, MLA_HEADS * V_HEAD, D), DEEPNORM_BETA * (MLA_HEADS * V_HEAD) ** -0.5),
        "ret_w_in": nrm((N_C, D, 2 * RET_HEADS * RET_QK + 2 * RET_HEADS * RET_V), D ** -0.5),
        "ret_gn_g": gain((N_C, RET_HEADS * RET_V)),
        "ret_w_out": nrm((N_C, RET_HEADS * RET_V, D), DEEPNORM_BETA * (RET_HEADS * RET_V) ** -0.5),
    }


def reference(x, positions, ln1_g, ln1_b, ln2_g, ln2_b, ffn_w_up, ffn_conv_w, ffn_conv_b, ffn_w_down,
              lru_w_in, lru_conv_w, lru_conv_b, lru_w_a, lru_b_a, lru_w_x, lru_b_x, lru_lambda, lru_w_out,
              mla_w_in, mla_q_norm, mla_kv_norm, mla_w_uq, mla_w_ukv, mla_w_out,
              ret_w_in, ret_gn_g, ret_w_out):
    for i in range(DEPTH):
        kind, j = i % N_MIXERS, i // N_MIXERS
        if kind == 0:
            mix = rglru_mixer(x, lru_w_in[j], lru_conv_w[j], lru_conv_b[j], lru_w_a[j], lru_b_a[j],
                              lru_w_x[j], lru_b_x[j], lru_lambda[j], lru_w_out[j])
        elif kind == 1:
            mix = mla_mixer(x, positions, mla_w_in[j], mla_q_norm[j], mla_kv_norm[j],
                            mla_w_uq[j], mla_w_ukv[j], mla_w_out[j])
        else:
            mix = retention_mixer(x, positions, ret_w_in[j], ret_gn_g[j], ret_w_out[j])
        x = layer_norm(DEEPNORM_ALPHA * x + mix.astype(x.dtype), ln1_g[i], ln1_b[i]).astype(x.dtype)
        f = conv_gated_ffn(x, ffn_w_up[i], ffn_conv_w[i], ffn_conv_b[i], ffn_w_down[i])
        x = layer_norm(DEEPNORM_ALPHA * x + f.astype(x.dtype), ln2_g[i], ln2_b[i]).astype(x.dtype)
    return x
```

```python
import functools
import math

import jax
import jax.numpy as jnp
from jax import lax
from jax.experimental import pallas as pl
from jax.experimental.pallas import tpu as pltpu

F32 = jnp.float32
BF16 = jnp.bfloat16

DEPTH = 4
N_MIXERS = 3
CHUNK = 64
DEEPNORM_ALPHA = (2.0 * DEPTH) ** 0.25
LN_EPS = 1e-5
RMS_EPS = 1e-6
ROPE_BASE = 10000.0

LRU_BLOCKS = 4
LRU_CONV_W = 4
LRU_C = 8.0

MLA_HEADS = 16
QK_NOPE = 64
QK_ROPE = 32
V_HEAD = 64
KV_LORA = 256
Q_LORA = 768

RET_HEADS = 4
FFN_CONV_W = 3

LANES = 128
SUBLANES = 8
VMEM_LIMIT = 56 * 1024 * 1024

FFN_TS = 512
FFN_FC = 256
LRU_TS = 512
MLA_TS = 512
ATT_T = 512
RET_TB = 256
PROJ_TS = 512


def _resident(shape):
    nd = len(shape)
    return pl.BlockSpec(shape, lambda *_: (0,) * nd, pipeline_mode=pl.Buffered(1))


def _rows(ts, width):
    return pl.BlockSpec((ts, width), lambda i: (i, 0))


def _params(n_axes=1):
    return pltpu.CompilerParams(dimension_semantics=("arbitrary",) * n_axes,
                                vmem_limit_bytes=VMEM_LIMIT)


def _dot(a, b):
    return jnp.dot(a, b, preferred_element_type=F32)


def _dot_nt(a, b):
    return lax.dot_general(a, b, (((1,), (1,)), ((), ())), preferred_element_type=F32)


def _dot_tn(a, b):
    return lax.dot_general(a, b, (((0,), (0,)), ((), ())), preferred_element_type=F32)


def _gelu(x):
    c = math.sqrt(2.0 / math.pi)
    return x * (0.5 * (1.0 + jnp.tanh(c * (x + 0.044715 * (x * x * x)))))


def _sigmoid(x):
    return 1.0 / (1.0 + jnp.exp(-x))


def _deepnorm_ln(x, mix, g, b):
    y = DEEPNORM_ALPHA * x + mix
    mu = jnp.mean(y, axis=-1, keepdims=True)
    d = y - mu
    var = jnp.mean(d * d, axis=-1, keepdims=True)
    return d * lax.rsqrt(var + LN_EPS) * g + b


def _ffn_kernel(x_ref, wg_ref, wu_ref, cw_ref, cb_ref, wd_ref, g_ref, b_ref, o_ref,
                carry_g, carry_u, work_g, work_u, gated_ref, *, ts, fc, n_chunks, f):
    @pl.when(pl.program_id(0) == 0)
    def _():
        carry_g[...] = jnp.zeros_like(carry_g)
        carry_u[...] = jnp.zeros_like(carry_u)

    x = x_ref[...]
    xb = x.astype(BF16)
    for c in range(n_chunks):
        lo, hi = c * fc, (c + 1) * fc
        slot = c % 2
        for w_ref, carry, work, off in ((wg_ref, carry_g, work_g, 0), (wu_ref, carry_u, work_u, f)):
            h = _dot(xb, w_ref[:, lo:hi])
            work[slot, 0:SUBLANES, :] = carry[c]
            work[slot, SUBLANES:SUBLANES + ts, :] = h
            carry[c] = h[ts - SUBLANES:ts, :]
        conv = []
        for work, off in ((work_g, 0), (work_u, f)):
            y = cb_ref[:, off + lo:off + hi]
            for k in range(FFN_CONV_W):
                start = SUBLANES - (FFN_CONV_W - 1) + k
                y = y + cw_ref[k:k + 1, off + lo:off + hi] * work[slot, start:start + ts, :]
            conv.append(y)
        gated_ref[:, lo:hi] = (_gelu(conv[0]) * conv[1]).astype(BF16)
    ff = _dot(gated_ref[...], wd_ref[...])
    o_ref[...] = _deepnorm_ln(x, ff, g_ref[...], b_ref[...])


def _ffn_layer(x, w_up, conv_w, conv_b, w_down, ln_g, ln_b):
    s, d = x.shape
    f = w_down.shape[0]
    ts, fc = min(FFN_TS, s), FFN_FC
    n_chunks = f // fc
    assert f % fc == 0 and s % ts == 0
    wg = w_up[:, :f].astype(BF16)
    wu = w_up[:, f:].astype(BF16)
    kern = functools.partial(_ffn_kernel, ts=ts, fc=fc, n_chunks=n_chunks, f=f)
    return pl.pallas_call(
        kern,
        out_shape=jax.ShapeDtypeStruct((s, d), F32),
        grid=(s // ts,),
        in_specs=[_rows(ts, d), _resident((d, f)), _resident((d, f)),
                  _resident((FFN_CONV_W, 2 * f)), _resident((1, 2 * f)),
                  _resident((f, d)), _resident((1, d)), _resident((1, d))],
        out_specs=_rows(ts, d),
        scratch_shapes=[pltpu.VMEM((n_chunks, SUBLANES, fc), F32),
                        pltpu.VMEM((n_chunks, SUBLANES, fc), F32),
                        pltpu.VMEM((2, SUBLANES + ts, fc), F32),
                        pltpu.VMEM((2, SUBLANES + ts, fc), F32),
                        pltpu.VMEM((ts, f), BF16)],
        compiler_params=_params(),
        name="ffn_ln",
    )(x, wg, wu, conv_w, conv_b.reshape(1, -1), w_down.astype(BF16),
      ln_g.reshape(1, -1), ln_b.reshape(1, -1))


def _lru_kernel(x_ref, wgate_ref, wrnn_ref, cw_ref, cb_ref, wa_ref, ba_ref, wx_ref, bx_ref,
                lam_ref, wo_ref, g_ref, b_ref, o_ref,
                ubuf, a_buf, b_buf, h_buf, hcarry, *, ts, w):
    @pl.when(pl.program_id(0) == 0)
    def _():
        ubuf[0:SUBLANES, :] = jnp.zeros((SUBLANES, w), F32)
        hcarry[...] = jnp.zeros_like(hcarry)

    x = x_ref[...]
    xb = x.astype(BF16)
    gate_branch = _dot(xb, wgate_ref[...])
    rnn_in = _dot(xb, wrnn_ref[...])

    ubuf[SUBLANES:SUBLANES + ts, :] = rnn_in
    u = cb_ref[...]
    for k in range(LRU_CONV_W):
        start = SUBLANES - (LRU_CONV_W - 1) + k
        u = u + cw_ref[k:k + 1, :] * ubuf[start:start + ts, :]
    ubuf[0:SUBLANES, :] = rnn_in[ts - SUBLANES:ts, :]

    ub = u.astype(BF16)
    bw = w // LRU_BLOCKS
    r = jnp.concatenate([_dot(ub[:, g * bw:(g + 1) * bw], wa_ref[g]) for g in range(LRU_BLOCKS)], axis=1)
    i = jnp.concatenate([_dot(ub[:, g * bw:(g + 1) * bw], wx_ref[g]) for g in range(LRU_BLOCKS)], axis=1)
    r = _sigmoid(r + ba_ref[...])
    i = _sigmoid(i + bx_ref[...])

    neg_lam = -lam_ref[...]
    softplus = jnp.maximum(neg_lam, 0.0) + jnp.log(1.0 + jnp.exp(-jnp.abs(neg_lam)))
    log_a = (-LRU_C) * r * softplus
    a = jnp.exp(log_a)
    bv = jnp.sqrt(1.0 - jnp.exp(2.0 * log_a)) * (i * u)

    row = lax.broadcasted_iota(jnp.int32, (ts, w), 0) % SUBLANES
    for d in (1, 2, 4):
        keep = row >= d
        a_sh = jnp.where(keep, pltpu.roll(a, d, 0), 1.0)
        b_sh = jnp.where(keep, pltpu.roll(bv, d, 0), 0.0)
        bv = a * b_sh + bv
        a = a * a_sh
    a_buf[...] = a
    b_buf[...] = bv

    def tile_step(t, carry):
        r0 = pl.multiple_of(t * SUBLANES, SUBLANES)
        h = a_buf[pl.ds(r0, SUBLANES), :] * carry + b_buf[pl.ds(r0, SUBLANES), :]
        h_buf[pl.ds(r0, SUBLANES), :] = h
        return jnp.broadcast_to(h[SUBLANES - 1:SUBLANES, :], (SUBLANES, w))

    hcarry[...] = lax.fori_loop(0, ts // SUBLANES, tile_step, hcarry[...])

    y = (_gelu(gate_branch) * h_buf[...]).astype(BF16)
    o_ref[...] = _deepnorm_ln(x, _dot(y, wo_ref[...]), g_ref[...], b_ref[...])


def _lru_layer(x, w_in, conv_w, conv_b, w_a, b_a, w_x, b_x, lam, w_out, ln_g, ln_b):
    s, d = x.shape
    w = w_out.shape[0]
    ts = min(LRU_TS, s)
    assert s % ts == 0
    bw = w // LRU_BLOCKS
    kern = functools.partial(_lru_kernel, ts=ts, w=w)
    return pl.pallas_call(
        kern,
        out_shape=jax.ShapeDtypeStruct((s, d), F32),
        grid=(s // ts,),
        in_specs=[_rows(ts, d), _resident((d, w)), _resident((d, w)),
                  _resident((LRU_CONV_W, w)), _resident((1, w)),
                  _resident((LRU_BLOCKS, bw, bw)), _resident((1, w)),
                  _resident((LRU_BLOCKS, bw, bw)), _resident((1, w)),
                  _resident((1, w)), _resident((w, d)), _resident((1, d)), _resident((1, d))],
        out_specs=_rows(ts, d),
        scratch_shapes=[pltpu.VMEM((SUBLANES + ts, w), F32),
                        pltpu.VMEM((ts, w), F32), pltpu.VMEM((ts, w), F32), pltpu.VMEM((ts, w), F32),
                        pltpu.VMEM((SUBLANES, w), F32)],
        compiler_params=_params(),
        name="rglru_ln",
    )(x, w_in[:, :w].astype(BF16), w_in[:, w:].astype(BF16), conv_w, conv_b.reshape(1, -1),
      w_a.astype(BF16), b_a.reshape(1, -1), w_x.astype(BF16), b_x.reshape(1, -1),
      lam.reshape(1, -1), w_out.astype(BF16), ln_g.reshape(1, -1), ln_b.reshape(1, -1))


HEAD_PAD = LANES
ROPE_LO = QK_NOPE
ROPE_HALF = QK_ROPE // 2


def _rope_group(v, cos, sin_up, sin_dn):
    return (v * cos + pltpu.roll(v, ROPE_HALF, 1) * sin_up
            + pltpu.roll(v, HEAD_PAD - ROPE_HALF, 1) * sin_dn)


def _mla_proj_kernel(x_ref, pos_ref, invf_ref, wq_ref, wkv_ref, wpe_ref, qg_ref, kvg_ref,
                     wuq_ref, wuk_ref, wuv_ref, q_ref, k_ref, v_ref, *, ts, q_scale):
    xb = x_ref[...].astype(BF16)
    c_q = _dot(xb, wq_ref[...])
    c_kv = _dot(xb, wkv_ref[...])
    k_pe = _dot(xb, wpe_ref[...])

    def rms(v, g):
        return (v * lax.rsqrt(jnp.mean(v * v, axis=-1, keepdims=True) + RMS_EPS) * g).astype(BF16)

    cqn = rms(c_q, qg_ref[...])
    ckvn = rms(c_kv, kvg_ref[...])

    ang = pos_ref[...] * invf_ref[...]
    cos = jnp.cos(ang)
    sin = jnp.sin(ang)
    lane = lax.broadcasted_iota(jnp.int32, (ts, HEAD_PAD), 1)
    sin_up = jnp.where((lane >= ROPE_LO + ROPE_HALF) & (lane < ROPE_LO + QK_ROPE), sin, 0.0)
    sin_dn = jnp.where((lane >= ROPE_LO) & (lane < ROPE_LO + ROPE_HALF), -sin, 0.0)

    k_pe_r = _rope_group(k_pe, cos, sin_up, sin_dn)
    cos_q, sup_q, sdn_q = cos * q_scale, sin_up * q_scale, sin_dn * q_scale
    for h in range(MLA_HEADS):
        lo, hi = h * HEAD_PAD, (h + 1) * HEAD_PAD
        qh = _dot(cqn, wuq_ref[:, lo:hi])
        q_ref[:, lo:hi] = _rope_group(qh, cos_q, sup_q, sdn_q).astype(BF16)
        kh = _dot(ckvn, wuk_ref[:, lo:hi])
        k_ref[:, lo:hi] = (kh + k_pe_r).astype(BF16)
    v_ref[...] = _dot(ckvn, wuv_ref[...]).astype(BF16)


def _mla_proj(x, pos, w_in, q_norm_g, kv_norm_g, w_uq, w_ukv):
    s, d = x.shape
    ts = min(MLA_TS, s)
    assert s % ts == 0
    hp = MLA_HEADS * HEAD_PAD
    hv = MLA_HEADS * V_HEAD
    qk = QK_NOPE + QK_ROPE
    pad = HEAD_PAD - qk
    w_q = w_in[:, :Q_LORA].astype(BF16)
    w_kv = w_in[:, Q_LORA:Q_LORA + KV_LORA].astype(BF16)
    w_pe = jnp.pad(w_in[:, Q_LORA + KV_LORA:], ((0, 0), (ROPE_LO, pad))).astype(BF16)
    w_uq_p = jnp.pad(w_uq.reshape(Q_LORA, MLA_HEADS, qk), ((0, 0), (0, 0), (0, pad)))
    w_uq_p = w_uq_p.reshape(Q_LORA, hp).astype(BF16)
    w_ukv3 = w_ukv.reshape(KV_LORA, MLA_HEADS, QK_NOPE + V_HEAD)
    w_uk_p = jnp.pad(w_ukv3[:, :, :QK_NOPE], ((0, 0), (0, 0), (0, HEAD_PAD - QK_NOPE)))
    w_uk_p = w_uk_p.reshape(KV_LORA, hp).astype(BF16)
    w_uv = w_ukv3[:, :, QK_NOPE:].reshape(KV_LORA, hv).astype(BF16)
    inv_freq = ROPE_BASE ** (-jnp.arange(0, QK_ROPE, 2, dtype=F32) / QK_ROPE)
    invf = jnp.concatenate([jnp.zeros((ROPE_LO,), F32), inv_freq, inv_freq, jnp.zeros((pad,), F32)])
    q_scale = qk ** -0.5 * math.log2(math.e)
    kern = functools.partial(_mla_proj_kernel, ts=ts, q_scale=q_scale)
    return pl.pallas_call(
        kern,
        out_shape=(jax.ShapeDtypeStruct((s, hp), BF16), jax.ShapeDtypeStruct((s, hp), BF16),
                   jax.ShapeDtypeStruct((s, hv), BF16)),
        grid=(s // ts,),
        in_specs=[_rows(ts, d), _rows(ts, 1), _resident((1, HEAD_PAD)),
                  _resident((d, Q_LORA)), _resident((d, KV_LORA)), _resident((d, HEAD_PAD)),
                  _resident((1, Q_LORA)), _resident((1, KV_LORA)),
                  _resident((Q_LORA, hp)), _resident((KV_LORA, hp)), _resident((KV_LORA, hv))],
        out_specs=(_rows(ts, hp), _rows(ts, hp), _rows(ts, hv)),
        compiler_params=_params(),
        name="mla_proj",
    )(x, pos, invf.reshape(1, -1), w_q, w_kv, w_pe, q_norm_g.reshape(1, -1), kv_norm_g.reshape(1, -1),
      w_uq_p, w_uk_p, w_uv)


HEADS_PER_STEP = 2


def _attn_kernel(q_ref, k_ref, v_ref, o_ref, m_ref, l_ref, acc_ref, *, t):
    i = pl.program_id(1)
    m_ref[...] = jnp.full(m_ref.shape, -jnp.inf, F32)
    l_ref[...] = jnp.zeros_like(l_ref)
    acc_ref[...] = jnp.zeros_like(acc_ref)

    def tile(j, masked):
        r0 = pl.multiple_of(j * t, t)
        v = v_ref[pl.ds(r0, t), :]
        for h in range(HEADS_PER_STEP):
            lo, hi = h * HEAD_PAD, (h + 1) * HEAD_PAD
            s = _dot_nt(q_ref[:, lo:hi], k_ref[pl.ds(r0, t), lo:hi])
            if masked:
                qc = lax.broadcasted_iota(jnp.int32, (t, t), 0) // CHUNK
                kc = lax.broadcasted_iota(jnp.int32, (t, t), 1) // CHUNK
                s = jnp.where(kc <= qc, s, -jnp.inf)
            m_old = m_ref[h]
            m_new = jnp.maximum(m_old, jnp.max(s, axis=-1, keepdims=True))
            alpha = jnp.exp2(m_old - m_new)
            p = jnp.exp2(s - m_new)
            l_ref[h] = alpha * l_ref[h] + jnp.sum(p, axis=-1, keepdims=True)
            acc_ref[h] = alpha * acc_ref[h] + _dot(p.astype(BF16), v)
            m_ref[h] = m_new

    def body(j, carry):
        tile(j, False)
        return carry

    lax.fori_loop(0, i, body, 0)
    tile(i, True)

    lane = lax.broadcasted_iota(jnp.int32, (t, HEADS_PER_STEP * V_HEAD), 1)
    o0 = acc_ref[0] / l_ref[0]
    o1 = acc_ref[1] / l_ref[1]
    o_ref[...] = jnp.where(lane < V_HEAD, o0, o1).astype(BF16)


def _mla_attention(q, k, v):
    s = q.shape[0]
    t = min(ATT_T, s)
    assert s % t == 0 and t % CHUNK == 0
    n_pairs = MLA_HEADS // HEADS_PER_STEP
    qw = HEADS_PER_STEP * HEAD_PAD
    vw = HEADS_PER_STEP * V_HEAD
    kern = functools.partial(_attn_kernel, t=t)
    return pl.pallas_call(
        kern,
        out_shape=jax.ShapeDtypeStruct((s, MLA_HEADS * V_HEAD), BF16),
        grid=(n_pairs, s // t),
        in_specs=[pl.BlockSpec((t, qw), lambda p, i: (i, p)),
                  pl.BlockSpec((s, qw), lambda p, i: (0, p)),
                  pl.BlockSpec((s, vw), lambda p, i: (0, p))],
        out_specs=pl.BlockSpec((t, vw), lambda p, i: (i, p)),
        scratch_shapes=[pltpu.VMEM((HEADS_PER_STEP, t, 1), F32),
                        pltpu.VMEM((HEADS_PER_STEP, t, 1), F32),
                        pltpu.VMEM((HEADS_PER_STEP, t, vw), F32)],
        compiler_params=_params(2),
        name="mla_attention",
    )(q, k, v)


def _proj_ln_kernel(x_ref, y_ref, w_ref, g_ref, b_ref, o_ref):
    o_ref[...] = _deepnorm_ln(x_ref[...], _dot(y_ref[...], w_ref[...]), g_ref[...], b_ref[...])


def _proj_ln(x, y, w, ln_g, ln_b):
    s, d = x.shape
    kdim = y.shape[1]
    ts = min(PROJ_TS, s)
    assert s % ts == 0
    return pl.pallas_call(
        _proj_ln_kernel,
        out_shape=jax.ShapeDtypeStruct((s, d), F32),
        grid=(s // ts,),
        in_specs=[_rows(ts, d), _rows(ts, kdim), _resident((kdim, d)), _resident((1, d)), _resident((1, d))],
        out_specs=_rows(ts, d),
        compiler_params=_params(),
        name="proj_ln",
    )(x, y, w.astype(BF16), ln_g.reshape(1, -1), ln_b.reshape(1, -1))


def _ret_kernel(x_ref, pos_ref, invf_ref, wq_ref, wk_ref, wv_ref, wg_ref, dmat_ref, xi_ref, zeta_ref,
                gn_ref, wo_ref, g_ref, b_ref, o_ref, state_ref, gated_ref,
                *, tb, dqk, dv, k_scale, block_decay):
    @pl.when(pl.program_id(0) == 0)
    def _():
        state_ref[...] = jnp.zeros_like(state_ref)

    x = x_ref[...]
    xb = x.astype(BF16)
    half = dqk // 2
    ang = pos_ref[...] * invf_ref[...]
    cos = jnp.cos(ang)
    sin = jnp.sin(ang)

    def rope(v):
        v1, v2 = v[:, :half], v[:, half:]
        return jnp.concatenate([v1 * cos - v2 * sin, v1 * sin + v2 * cos], axis=1)

    for h in range(RET_HEADS):
        qh = rope(_dot(xb, wq_ref[:, h * dqk:(h + 1) * dqk]))
        kh = rope(_dot(xb, wk_ref[:, h * dqk:(h + 1) * dqk])) * k_scale
        vh = _dot(xb, wv_ref[:, h * dv:(h + 1) * dv]).astype(BF16)
        gh = _dot(xb, wg_ref[:, h * dv:(h + 1) * dv])
        qb = qh.astype(BF16)
        sc = _dot_nt(qb, kh.astype(BF16)) * dmat_ref[h]
        state = state_ref[h]
        y = _dot(sc.astype(BF16), vh) + _dot(qb, state.astype(BF16)) * xi_ref[h]
        state_ref[h] = state * block_decay[h] + _dot_tn((kh * zeta_ref[h]).astype(BF16), vh)
        mu = jnp.mean(y, axis=-1, keepdims=True)
        dlt = y - mu
        var = jnp.mean(dlt * dlt, axis=-1, keepdims=True)
        yn = dlt * lax.rsqrt(var + LN_EPS) * gn_ref[:, h * dv:(h + 1) * dv]
        gated_ref[:, h * dv:(h + 1) * dv] = (gh * _sigmoid(gh) * yn).astype(BF16)
    o_ref[...] = _deepnorm_ln(x, _dot(gated_ref[...], wo_ref[...]), g_ref[...], b_ref[...])


def _ret_layer(x, pos, w_in, gn_g, w_out, ln_g, ln_b):
    s, d = x.shape
    hv = w_out.shape[0]
    dv = hv // RET_HEADS
    hq = (w_in.shape[1] - 2 * hv) // 2
    dqk = hq // RET_HEADS
    tb = min(RET_TB, s)
    assert s % tb == 0 and tb % CHUNK == 0
    half = dqk // 2
    inv_freq = ROPE_BASE ** (-jnp.arange(0, dqk, 2, dtype=F32) / dqk)
    log_gamma = jnp.log1p(-jnp.exp2(-5.0 - jnp.arange(RET_HEADS, dtype=F32)))
    idx = jnp.arange(tb, dtype=F32)
    rel = jnp.abs(idx[:, None] - idx[None, :])
    ic = jnp.arange(tb) // CHUNK
    visible = ic[None, :] <= ic[:, None]
    dmat = jnp.where(visible[None], jnp.exp(log_gamma[:, None, None] * rel), 0.0)
    xi = jnp.exp(log_gamma[:, None] * (idx + 1.0)[None, :])[:, :, None]
    zeta = jnp.exp(log_gamma[:, None] * (tb - 1.0 - idx)[None, :])[:, :, None]
    block_decay = tuple(float(math.exp(math.log1p(-2.0 ** (-5.0 - h)) * tb)) for h in range(RET_HEADS))
    kern = functools.partial(_ret_kernel, tb=tb, dqk=dqk, dv=dv, k_scale=dqk ** -0.5,
                             block_decay=block_decay)
    return pl.pallas_call(
        kern,
        out_shape=jax.ShapeDtypeStruct((s, d), F32),
        grid=(s // tb,),
        in_specs=[_rows(tb, d), _rows(tb, 1), _resident((1, half)),
                  _resident((d, hq)), _resident((d, hq)), _resident((d, hv)), _resident((d, hv)),
                  _resident((RET_HEADS, tb, tb)), _resident((RET_HEADS, tb, 1)), _resident((RET_HEADS, tb, 1)),
                  _resident((1, hv)), _resident((hv, d)), _resident((1, d)), _resident((1, d))],
        out_specs=_rows(tb, d),
        scratch_shapes=[pltpu.VMEM((RET_HEADS, dqk, dv), F32), pltpu.VMEM((tb, hv), BF16)],
        compiler_params=_params(),
        name="retention_ln",
    )(x, pos, inv_freq.reshape(1, -1),
      w_in[:, :hq].astype(BF16), w_in[:, hq:2 * hq].astype(BF16),
      w_in[:, 2 * hq:2 * hq + hv].astype(BF16), w_in[:, 2 * hq + hv:].astype(BF16),
      dmat, xi, zeta, gn_g.reshape(1, -1), w_out.astype(BF16), ln_g.reshape(1, -1), ln_b.reshape(1, -1))


def kernel(x, positions, ln1_g, ln1_b, ln2_g, ln2_b, ffn_w_up, ffn_conv_w, ffn_conv_b, ffn_w_down,
           lru_w_in, lru_conv_w, lru_conv_b, lru_w_a, lru_b_a, lru_w_x, lru_b_x, lru_lambda, lru_w_out,
           mla_w_in, mla_q_norm, mla_kv_norm, mla_w_uq, mla_w_ukv, mla_w_out,
           ret_w_in, ret_gn_g, ret_w_out):
    bsz, s, d = x.shape
    outs = []
    for bi in range(bsz):
        h = x[bi]
        pos = positions[bi].astype(F32).reshape(s, 1)
        for i in range(DEPTH):
            kind, j = i % N_MIXERS, i // N_MIXERS
            if kind == 0:
                h = _lru_layer(h, lru_w_in[j], lru_conv_w[j], lru_conv_b[j], lru_w_a[j], lru_b_a[j],
                               lru_w_x[j], lru_b_x[j], lru_lambda[j], lru_w_out[j], ln1_g[i], ln1_b[i])
            elif kind == 1:
                q, k, v = _mla_proj(h, pos, mla_w_in[j], mla_q_norm[j], mla_kv_norm[j],
                                    mla_w_uq[j], mla_w_ukv[j])
                o = _mla_attention(q, k, v)
                h = _proj_ln(h, o, mla_w_out[j], ln1_g[i], ln1_b[i])
            else:
                h = _ret_layer(h, pos, ret_w_in[j], ret_gn_g[j], ret_w_out[j], ln1_g[i], ln1_b[i])
            h = _ffn_layer(h, ffn_w_up[i], ffn_conv_w[i], ffn_conv_b[i], ffn_w_down[i], ln2_g[i], ln2_b[i])
        outs.append(h)
    return jnp.stack(outs, axis=0)
```

```python
import functools
import math

import jax
import jax.numpy as jnp
from jax import lax
from jax.experimental import pallas as pl
from jax.experimental.pallas import tpu as pltpu

F32 = jnp.float32
BF16 = jnp.bfloat16

DEPTH = 4
N_MIXERS = 3
CHUNK = 64
DEEPNORM_ALPHA = (2.0 * DEPTH) ** 0.25
LN_EPS = 1e-5
RMS_EPS = 1e-6
ROPE_BASE = 10000.0

LRU_BLOCKS = 4
LRU_CONV_W = 4
LRU_C = 8.0

MLA_HEADS = 16
QK_NOPE = 64
QK_ROPE = 32
V_HEAD = 64
KV_LORA = 256
Q_LORA = 768

RET_HEADS = 4
FFN_CONV_W = 3

LANES = 128
SUBLANES = 8
VMEM_LIMIT = 56 * 1024 * 1024

FFN_TS = 512
FFN_FC = 256
LRU_TS = 512
MLA_TS = 512
ATT_TQ = 1024
ATT_TK = 1024
RET_TB = 256
PROJ_TS = 512


def _resident(shape):
    nd = len(shape)
    return pl.BlockSpec(shape, lambda *_: (0,) * nd, pipeline_mode=pl.Buffered(1))


def _rows(ts, width):
    return pl.BlockSpec((ts, width), lambda i: (i, 0))


def _params(n_axes=1):
    return pltpu.CompilerParams(dimension_semantics=("arbitrary",) * n_axes,
                                vmem_limit_bytes=VMEM_LIMIT)


def _dot(a, b):
    return jnp.dot(a, b, preferred_element_type=F32)


def _dot_nt(a, b):
    return lax.dot_general(a, b, (((1,), (1,)), ((), ())), preferred_element_type=F32)


def _dot_tn(a, b):
    return lax.dot_general(a, b, (((0,), (0,)), ((), ())), preferred_element_type=F32)


def _gelu(x):
    c = math.sqrt(2.0 / math.pi)
    return x * (0.5 * (1.0 + jnp.tanh(c * (x + 0.044715 * (x * x * x)))))


def _sigmoid(x):
    return 1.0 / (1.0 + jnp.exp(-x))


def _deepnorm_ln(x, mix, g, b):
    y = DEEPNORM_ALPHA * x + mix
    mu = jnp.mean(y, axis=-1, keepdims=True)
    d = y - mu
    var = jnp.mean(d * d, axis=-1, keepdims=True)
    return d * lax.rsqrt(var + LN_EPS) * g + b


def _ffn_kernel(x_ref, wg_ref, wu_ref, cw_ref, cb_ref, wd_ref, g_ref, b_ref, o_ref,
                carry_g, carry_u, work_g, work_u, gated_ref, *, ts, fc, n_chunks, f):
    @pl.when(pl.program_id(0) == 0)
    def _():
        carry_g[...] = jnp.zeros_like(carry_g)
        carry_u[...] = jnp.zeros_like(carry_u)

    x = x_ref[...]
    xb = x.astype(BF16)
    for c in range(n_chunks):
        lo, hi = c * fc, (c + 1) * fc
        slot = c % 2
        for w_ref, carry, work, off in ((wg_ref, carry_g, work_g, 0), (wu_ref, carry_u, work_u, f)):
            h = _dot(xb, w_ref[:, lo:hi])
            work[slot, 0:SUBLANES, :] = carry[c]
            work[slot, SUBLANES:SUBLANES + ts, :] = h
            carry[c] = h[ts - SUBLANES:ts, :]
        conv = []
        for work, off in ((work_g, 0), (work_u, f)):
            y = cb_ref[:, off + lo:off + hi]
            for k in range(FFN_CONV_W):
                start = SUBLANES - (FFN_CONV_W - 1) + k
                y = y + cw_ref[k:k + 1, off + lo:off + hi] * work[slot, start:start + ts, :]
            conv.append(y)
        gated_ref[:, lo:hi] = (_gelu(conv[0]) * conv[1]).astype(BF16)
    ff = _dot(gated_ref[...], wd_ref[...])
    o_ref[...] = _deepnorm_ln(x, ff, g_ref[...], b_ref[...])


def _ffn_layer(x, w_up, conv_w, conv_b, w_down, ln_g, ln_b):
    s, d = x.shape
    f = w_down.shape[0]
    ts, fc = min(FFN_TS, s), FFN_FC
    n_chunks = f // fc
    assert f % fc == 0 and s % ts == 0
    wg = w_up[:, :f].astype(BF16)
    wu = w_up[:, f:].astype(BF16)
    kern = functools.partial(_ffn_kernel, ts=ts, fc=fc, n_chunks=n_chunks, f=f)
    return pl.pallas_call(
        kern,
        out_shape=jax.ShapeDtypeStruct((s, d), F32),
        grid=(s // ts,),
        in_specs=[_rows(ts, d), _resident((d, f)), _resident((d, f)),
                  _resident((FFN_CONV_W, 2 * f)), _resident((1, 2 * f)),
                  _resident((f, d)), _resident((1, d)), _resident((1, d))],
        out_specs=_rows(ts, d),
        scratch_shapes=[pltpu.VMEM((n_chunks, SUBLANES, fc), F32),
                        pltpu.VMEM((n_chunks, SUBLANES, fc), F32),
                        pltpu.VMEM((2, SUBLANES + ts, fc), F32),
                        pltpu.VMEM((2, SUBLANES + ts, fc), F32),
                        pltpu.VMEM((ts, f), BF16)],
        compiler_params=_params(),
        name="ffn_ln",
    )(x, wg, wu, conv_w, conv_b.reshape(1, -1), w_down.astype(BF16),
      ln_g.reshape(1, -1), ln_b.reshape(1, -1))


def _lru_kernel(x_ref, wgate_ref, wrnn_ref, cw_ref, cb_ref, wa_ref, ba_ref, wx_ref, bx_ref,
                lam_ref, wo_ref, g_ref, b_ref, o_ref,
                ubuf, a_buf, b_buf, h_buf, hcarry, *, ts, w):
    @pl.when(pl.program_id(0) == 0)
    def _():
        ubuf[0:SUBLANES, :] = jnp.zeros((SUBLANES, w), F32)
        hcarry[...] = jnp.zeros_like(hcarry)

    x = x_ref[...]
    xb = x.astype(BF16)
    gate_branch = _dot(xb, wgate_ref[...])
    rnn_in = _dot(xb, wrnn_ref[...])

    ubuf[SUBLANES:SUBLANES + ts, :] = rnn_in
    u = cb_ref[...]
    for k in range(LRU_CONV_W):
        start = SUBLANES - (LRU_CONV_W - 1) + k
        u = u + cw_ref[k:k + 1, :] * ubuf[start:start + ts, :]
    ubuf[0:SUBLANES, :] = rnn_in[ts - SUBLANES:ts, :]

    ub = u.astype(BF16)
    bw = w // LRU_BLOCKS
    r = jnp.concatenate([_dot(ub[:, g * bw:(g + 1) * bw], wa_ref[g]) for g in range(LRU_BLOCKS)], axis=1)
    i = jnp.concatenate([_dot(ub[:, g * bw:(g + 1) * bw], wx_ref[g]) for g in range(LRU_BLOCKS)], axis=1)
    r = _sigmoid(r + ba_ref[...])
    i = _sigmoid(i + bx_ref[...])

    neg_lam = -lam_ref[...]
    softplus = jnp.maximum(neg_lam, 0.0) + jnp.log(1.0 + jnp.exp(-jnp.abs(neg_lam)))
    log_a = (-LRU_C) * r * softplus
    a = jnp.exp(log_a)
    bv = jnp.sqrt(1.0 - jnp.exp(2.0 * log_a)) * (i * u)

    row = lax.broadcasted_iota(jnp.int32, (ts, w), 0) % SUBLANES
    for d in (1, 2, 4):
        keep = row >= d
        a_sh = jnp.where(keep, pltpu.roll(a, d, 0), 1.0)
        b_sh = jnp.where(keep, pltpu.roll(bv, d, 0), 0.0)
        bv = a * b_sh + bv
        a = a * a_sh
    a_buf[...] = a
    b_buf[...] = bv

    def tile_step(t, carry):
        r0 = pl.multiple_of(t * SUBLANES, SUBLANES)
        h = a_buf[pl.ds(r0, SUBLANES), :] * carry + b_buf[pl.ds(r0, SUBLANES), :]
        h_buf[pl.ds(r0, SUBLANES), :] = h
        return jnp.broadcast_to(h[SUBLANES - 1:SUBLANES, :], (SUBLANES, w))

    hcarry[...] = lax.fori_loop(0, ts // SUBLANES, tile_step, hcarry[...])

    y = (_gelu(gate_branch) * h_buf[...]).astype(BF16)
    o_ref[...] = _deepnorm_ln(x, _dot(y, wo_ref[...]), g_ref[...], b_ref[...])


def _lru_layer(x, w_in, conv_w, conv_b, w_a, b_a, w_x, b_x, lam, w_out, ln_g, ln_b):
    s, d = x.shape
    w = w_out.shape[0]
    ts = min(LRU_TS, s)
    assert s % ts == 0
    bw = w // LRU_BLOCKS
    kern = functools.partial(_lru_kernel, ts=ts, w=w)
    return pl.pallas_call(
        kern,
        out_shape=jax.ShapeDtypeStruct((s, d), F32),
        grid=(s // ts,),
        in_specs=[_rows(ts, d), _resident((d, w)), _resident((d, w)),
                  _resident((LRU_CONV_W, w)), _resident((1, w)),
                  _resident((LRU_BLOCKS, bw, bw)), _resident((1, w)),
                  _resident((LRU_BLOCKS, bw, bw)), _resident((1, w)),
                  _resident((1, w)), _resident((w, d)), _resident((1, d)), _resident((1, d))],
        out_specs=_rows(ts, d),
        scratch_shapes=[pltpu.VMEM((SUBLANES + ts, w), F32),
                        pltpu.VMEM((ts, w), F32), pltpu.VMEM((ts, w), F32), pltpu.VMEM((ts, w), F32),
                        pltpu.VMEM((SUBLANES, w), F32)],
        compiler_params=_params(),
        name="rglru_ln",
    )(x, w_in[:, :w].astype(BF16), w_in[:, w:].astype(BF16), conv_w, conv_b.reshape(1, -1),
      w_a.astype(BF16), b_a.reshape(1, -1), w_x.astype(BF16), b_x.reshape(1, -1),
      lam.reshape(1, -1), w_out.astype(BF16), ln_g.reshape(1, -1), ln_b.reshape(1, -1))


HEAD_PAD = LANES
ROPE_LO = QK_NOPE
ROPE_HALF = QK_ROPE // 2


def _rope_group(v, cos, sin_up, sin_dn):
    return (v * cos + pltpu.roll(v, ROPE_HALF, 1) * sin_up
            + pltpu.roll(v, HEAD_PAD - ROPE_HALF, 1) * sin_dn)


def _mla_proj_kernel(x_ref, pos_ref, invf_ref, wq_ref, wkv_ref, wpe_ref, qg_ref, kvg_ref,
                     wuq_ref, wuk_ref, wuv_ref, vone_ref, q_ref, k_ref, v_ref, *, ts, q_scale):
    xb = x_ref[...].astype(BF16)
    c_q = _dot(xb, wq_ref[...])
    c_kv = _dot(xb, wkv_ref[...])
    k_pe = _dot(xb, wpe_ref[...])

    def rms(v, g):
        return (v * lax.rsqrt(jnp.mean(v * v, axis=-1, keepdims=True) + RMS_EPS) * g).astype(BF16)

    cqn = rms(c_q, qg_ref[...])
    ckvn = rms(c_kv, kvg_ref[...])

    ang = pos_ref[...] * invf_ref[...]
    cos = jnp.cos(ang)
    sin = jnp.sin(ang)
    lane = lax.broadcasted_iota(jnp.int32, (ts, HEAD_PAD), 1)
    sin_up = jnp.where((lane >= ROPE_LO + ROPE_HALF) & (lane < ROPE_LO + QK_ROPE), sin, 0.0)
    sin_dn = jnp.where((lane >= ROPE_LO) & (lane < ROPE_LO + ROPE_HALF), -sin, 0.0)

    k_pe_r = _rope_group(k_pe, cos, sin_up, sin_dn)
    cos_q, sup_q, sdn_q = cos * q_scale, sin_up * q_scale, sin_dn * q_scale
    for h in range(MLA_HEADS):
        lo, hi = h * HEAD_PAD, (h + 1) * HEAD_PAD
        qh = _dot(cqn, wuq_ref[:, lo:hi])
        q_ref[:, lo:hi] = _rope_group(qh, cos_q, sup_q, sdn_q).astype(BF16)
        kh = _dot(ckvn, wuk_ref[:, lo:hi])
        k_ref[:, lo:hi] = (kh + k_pe_r).astype(BF16)
    v_ref[...] = (_dot(ckvn, wuv_ref[...]) + vone_ref[...]).astype(BF16)


def _mla_proj(x, pos, w_in, q_norm_g, kv_norm_g, w_uq, w_ukv):
    s, d = x.shape
    ts = min(MLA_TS, s)
    assert s % ts == 0
    hp = MLA_HEADS * HEAD_PAD
    qk = QK_NOPE + QK_ROPE
    pad = HEAD_PAD - qk
    w_q = w_in[:, :Q_LORA].astype(BF16)
    w_kv = w_in[:, Q_LORA:Q_LORA + KV_LORA].astype(BF16)
    w_pe = jnp.pad(w_in[:, Q_LORA + KV_LORA:], ((0, 0), (ROPE_LO, pad))).astype(BF16)
    w_uq_p = jnp.pad(w_uq.reshape(Q_LORA, MLA_HEADS, qk), ((0, 0), (0, 0), (0, pad)))
    w_uq_p = w_uq_p.reshape(Q_LORA, hp).astype(BF16)
    w_ukv3 = w_ukv.reshape(KV_LORA, MLA_HEADS, QK_NOPE + V_HEAD)
    w_uk_p = jnp.pad(w_ukv3[:, :, :QK_NOPE], ((0, 0), (0, 0), (0, HEAD_PAD - QK_NOPE)))
    w_uk_p = w_uk_p.reshape(KV_LORA, hp).astype(BF16)
    w_uv3 = w_ukv3[:, :, QK_NOPE:].reshape(KV_LORA, MLA_HEADS // 2, 2, V_HEAD)
    zero = jnp.zeros_like(w_uv3[:, :, 0])
    w_uv_p = jnp.stack([w_uv3[:, :, 0], zero, zero, w_uv3[:, :, 1]], axis=2).reshape(KV_LORA, hp).astype(BF16)
    pair_ones = jnp.zeros((2 * HEAD_PAD,), F32).at[V_HEAD].set(1.0).at[HEAD_PAD].set(1.0)
    v_ones = jnp.tile(pair_ones, MLA_HEADS // 2).reshape(1, hp)
    inv_freq = ROPE_BASE ** (-jnp.arange(0, QK_ROPE, 2, dtype=F32) / QK_ROPE)
    invf = jnp.concatenate([jnp.zeros((ROPE_LO,), F32), inv_freq, inv_freq, jnp.zeros((pad,), F32)])
    q_scale = qk ** -0.5 * math.log2(math.e)
    kern = functools.partial(_mla_proj_kernel, ts=ts, q_scale=q_scale)
    return pl.pallas_call(
        kern,
        out_shape=(jax.ShapeDtypeStruct((s, hp), BF16), jax.ShapeDtypeStruct((s, hp), BF16),
                   jax.ShapeDtypeStruct((s, hp), BF16)),
        grid=(s // ts,),
        in_specs=[_rows(ts, d), _rows(ts, 1), _resident((1, HEAD_PAD)),
                  _resident((d, Q_LORA)), _resident((d, KV_LORA)), _resident((d, HEAD_PAD)),
                  _resident((1, Q_LORA)), _resident((1, KV_LORA)),
                  _resident((Q_LORA, hp)), _resident((KV_LORA, hp)), _resident((KV_LORA, hp)),
                  _resident((1, hp))],
        out_specs=(_rows(ts, hp), _rows(ts, hp), _rows(ts, hp)),
        compiler_params=_params(),
        name="mla_proj",
    )(x, pos, invf.reshape(1, -1), w_q, w_kv, w_pe, q_norm_g.reshape(1, -1), kv_norm_g.reshape(1, -1),
      w_uq_p, w_uk_p, w_uv_p, v_ones)


HEADS_PER_STEP = 2


def _attn_kernel(q_ref, k_ref, v_ref, o_ref, m_ref, acc_ref, *, tq, tk):
    i = pl.program_id(1)
    m_ref[...] = jnp.full(m_ref.shape, -jnp.inf, F32)
    acc_ref[...] = jnp.zeros_like(acc_ref)
    reps = tk // LANES

    def tile(j, diag_off):
        r0 = pl.multiple_of(j * tk, tk)
        for h in range(HEADS_PER_STEP):
            lo, hi = h * HEAD_PAD, (h + 1) * HEAD_PAD
            s = _dot_nt(q_ref[:, lo:hi], k_ref[pl.ds(r0, tk), lo:hi])
            if diag_off is not None:
                qc = lax.broadcasted_iota(jnp.int32, (tq, tk), 0) // CHUNK
                kc = (lax.broadcasted_iota(jnp.int32, (tq, tk), 1) + diag_off) // CHUNK
                s = jnp.where(kc <= qc, s, -jnp.inf)
            m_old = m_ref[h]
            m_new = jnp.maximum(m_old, jnp.max(s, axis=-1, keepdims=True))
            alpha = jnp.exp2(m_old - m_new)
            p = jnp.exp2(s - pltpu.repeat(m_new, reps, axis=1))
            acc_ref[h] = alpha * acc_ref[h] + _dot(p.astype(BF16), v_ref[pl.ds(r0, tk), lo:hi])
            m_ref[h] = m_new

    def body(j, carry):
        tile(j, None)
        return carry

    n_diag = tq // tk
    lax.fori_loop(0, i * n_diag, body, 0)
    for d in range(n_diag):
        tile(i * n_diag + d, d * tk)

    lane = lax.broadcasted_iota(jnp.int32, (tq, LANES), 1)
    a0 = acc_ref[0]
    a1 = acc_ref[1]
    o0 = a0 / a0[:, V_HEAD:V_HEAD + 1]
    o1 = a1 / a1[:, 0:1]
    o_ref[...] = jnp.where(lane < V_HEAD, o0, o1).astype(BF16)


def _mla_attention(q, k, v):
    s = q.shape[0]
    tq = min(ATT_TQ, s)
    tk = min(ATT_TK, tq)
    assert s % tq == 0 and tq % tk == 0 and tk % CHUNK == 0 and HEADS_PER_STEP * V_HEAD == LANES
    n_pairs = MLA_HEADS // HEADS_PER_STEP
    qw = HEADS_PER_STEP * HEAD_PAD
    kern = functools.partial(_attn_kernel, tq=tq, tk=tk)
    return pl.pallas_call(
        kern,
        out_shape=jax.ShapeDtypeStruct((s, MLA_HEADS * V_HEAD), BF16),
        grid=(n_pairs, s // tq),
        in_specs=[pl.BlockSpec((tq, qw), lambda p, i: (i, p)),
                  pl.BlockSpec((s, qw), lambda p, i: (0, p)),
                  pl.BlockSpec((s, qw), lambda p, i: (0, p))],
        out_specs=pl.BlockSpec((tq, LANES), lambda p, i: (i, p)),
        scratch_shapes=[pltpu.VMEM((HEADS_PER_STEP, tq, LANES), F32),
                        pltpu.VMEM((HEADS_PER_STEP, tq, LANES), F32)],
        compiler_params=_params(2),
        name="mla_attention",
    )(q, k, v)


def _proj_ln_kernel(x_ref, y_ref, w_ref, g_ref, b_ref, o_ref):
    o_ref[...] = _deepnorm_ln(x_ref[...], _dot(y_ref[...], w_ref[...]), g_ref[...], b_ref[...])


def _proj_ln(x, y, w, ln_g, ln_b):
    s, d = x.shape
    kdim = y.shape[1]
    ts = min(PROJ_TS, s)
    assert s % ts == 0
    return pl.pallas_call(
        _proj_ln_kernel,
        out_shape=jax.ShapeDtypeStruct((s, d), F32),
        grid=(s // ts,),
        in_specs=[_rows(ts, d), _rows(ts, kdim), _resident((kdim, d)), _resident((1, d)), _resident((1, d))],
        out_specs=_rows(ts, d),
        compiler_params=_params(),
        name="proj_ln",
    )(x, y, w.astype(BF16), ln_g.reshape(1, -1), ln_b.reshape(1, -1))


def _ret_kernel(x_ref, pos_ref, invf_ref, wq_ref, wk_ref, wv_ref, wg_ref, dmat_ref, xi_ref, zeta_ref,
                gn_ref, wo_ref, g_ref, b_ref, o_ref, state_ref, gated_ref,
                *, tb, dqk, dv, k_scale, block_decay):
    @pl.when(pl.program_id(0) == 0)
    def _():
        state_ref[...] = jnp.zeros_like(state_ref)

    x = x_ref[...]
    xb = x.astype(BF16)
    half = dqk // 2
    ang = pos_ref[...] * invf_ref[...]
    cos = jnp.cos(ang)
    sin = jnp.sin(ang)

    def rope(v):
        v1, v2 = v[:, :half], v[:, half:]
        return jnp.concatenate([v1 * cos - v2 * sin, v1 * sin + v2 * cos], axis=1)

    for h in range(RET_HEADS):
        qh = rope(_dot(xb, wq_ref[:, h * dqk:(h + 1) * dqk]))
        kh = rope(_dot(xb, wk_ref[:, h * dqk:(h + 1) * dqk])) * k_scale
        vh = _dot(xb, wv_ref[:, h * dv:(h + 1) * dv]).astype(BF16)
        gh = _dot(xb, wg_ref[:, h * dv:(h + 1) * dv])
        qb = qh.astype(BF16)
        sc = _dot_nt(qb, kh.astype(BF16)) * dmat_ref[h]
        state = state_ref[h]
        y = _dot(sc.astype(BF16), vh) + _dot(qb, state.astype(BF16)) * xi_ref[h]
        state_ref[h] = state * block_decay[h] + _dot_tn((kh * zeta_ref[h]).astype(BF16), vh)
        mu = jnp.mean(y, axis=-1, keepdims=True)
        dlt = y - mu
        var = jnp.mean(dlt * dlt, axis=-1, keepdims=True)
        yn = dlt * lax.rsqrt(var + LN_EPS) * gn_ref[:, h * dv:(h + 1) * dv]
        gated_ref[:, h * dv:(h + 1) * dv] = (gh * _sigmoid(gh) * yn).astype(BF16)
    o_ref[...] = _deepnorm_ln(x, _dot(gated_ref[...], wo_ref[...]), g_ref[...], b_ref[...])


def _ret_layer(x, pos, w_in, gn_g, w_out, ln_g, ln_b):
    s, d = x.shape
    hv = w_out.shape[0]
    dv = hv // RET_HEADS
    hq = (w_in.shape[1] - 2 * hv) // 2
    dqk = hq // RET_HEADS
    tb = min(RET_TB, s)
    assert s % tb == 0 and tb % CHUNK == 0
    half = dqk // 2
    inv_freq = ROPE_BASE ** (-jnp.arange(0, dqk, 2, dtype=F32) / dqk)
    log_gamma = jnp.log1p(-jnp.exp2(-5.0 - jnp.arange(RET_HEADS, dtype=F32)))
    idx = jnp.arange(tb, dtype=F32)
    rel = jnp.abs(idx[:, None] - idx[None, :])
    ic = jnp.arange(tb) // CHUNK
    visible = ic[None, :] <= ic[:, None]
    dmat = jnp.where(visible[None], jnp.exp(log_gamma[:, None, None] * rel), 0.0)
    xi = jnp.exp(log_gamma[:, None] * (idx + 1.0)[None, :])[:, :, None]
    zeta = jnp.exp(log_gamma[:, None] * (tb - 1.0 - idx)[None, :])[:, :, None]
    block_decay = tuple(float(math.exp(math.log1p(-2.0 ** (-5.0 - h)) * tb)) for h in range(RET_HEADS))
    kern = functools.partial(_ret_kernel, tb=tb, dqk=dqk, dv=dv, k_scale=dqk ** -0.5,
                             block_decay=block_decay)
    return pl.pallas_call(
        kern,
        out_shape=jax.ShapeDtypeStruct((s, d), F32),
        grid=(s // tb,),
        in_specs=[_rows(tb, d), _rows(tb, 1), _resident((1, half)),
                  _resident((d, hq)), _resident((d, hq)), _resident((d, hv)), _resident((d, hv)),
                  _resident((RET_HEADS, tb, tb)), _resident((RET_HEADS, tb, 1)), _resident((RET_HEADS, tb, 1)),
                  _resident((1, hv)), _resident((hv, d)), _resident((1, d)), _resident((1, d))],
        out_specs=_rows(tb, d),
        scratch_shapes=[pltpu.VMEM((RET_HEADS, dqk, dv), F32), pltpu.VMEM((tb, hv), BF16)],
        compiler_params=_params(),
        name="retention_ln",
    )(x, pos, inv_freq.reshape(1, -1),
      w_in[:, :hq].astype(BF16), w_in[:, hq:2 * hq].astype(BF16),
      w_in[:, 2 * hq:2 * hq + hv].astype(BF16), w_in[:, 2 * hq + hv:].astype(BF16),
      dmat, xi, zeta, gn_g.reshape(1, -1), w_out.astype(BF16), ln_g.reshape(1, -1), ln_b.reshape(1, -1))


def kernel(x, positions, ln1_g, ln1_b, ln2_g, ln2_b, ffn_w_up, ffn_conv_w, ffn_conv_b, ffn_w_down,
           lru_w_in, lru_conv_w, lru_conv_b, lru_w_a, lru_b_a, lru_w_x, lru_b_x, lru_lambda, lru_w_out,
           mla_w_in, mla_q_norm, mla_kv_norm, mla_w_uq, mla_w_ukv, mla_w_out,
           ret_w_in, ret_gn_g, ret_w_out):
    bsz, s, d = x.shape
    outs = []
    for bi in range(bsz):
        h = x[bi]
        pos = positions[bi].astype(F32).reshape(s, 1)
        for i in range(DEPTH):
            kind, j = i % N_MIXERS, i // N_MIXERS
            if kind == 0:
                h = _lru_layer(h, lru_w_in[j], lru_conv_w[j], lru_conv_b[j], lru_w_a[j], lru_b_a[j],
                               lru_w_x[j], lru_b_x[j], lru_lambda[j], lru_w_out[j], ln1_g[i], ln1_b[i])
            elif kind == 1:
                q, k, v = _mla_proj(h, pos, mla_w_in[j], mla_q_norm[j], mla_kv_norm[j],
                                    mla_w_uq[j], mla_w_ukv[j])
                o = _mla_attention(q, k, v)
                h = _proj_ln(h, o, mla_w_out[j], ln1_g[i], ln1_b[i])
            else:
                h = _ret_layer(h, pos, ret_w_in[j], ret_gn_g[j], ret_w_out[j], ln1_g[i], ln1_b[i])
            h = _ffn_layer(h, ffn_w_up[i], ffn_conv_w[i], ffn_conv_b[i], ffn_w_down[i], ln2_g[i], ln2_b[i])
        outs.append(h)
    return jnp.stack(outs, axis=0)
```

```python
import functools
import math

import jax
import jax.numpy as jnp
from jax import lax
from jax.experimental import pallas as pl
from jax.experimental.pallas import tpu as pltpu

F32 = jnp.float32
BF16 = jnp.bfloat16

DEPTH = 4
N_MIXERS = 3
CHUNK = 64
DEEPNORM_ALPHA = (2.0 * DEPTH) ** 0.25
LN_EPS = 1e-5
RMS_EPS = 1e-6
ROPE_BASE = 10000.0

LRU_BLOCKS = 4
LRU_CONV_W = 4
LRU_C = 8.0

MLA_HEADS = 16
QK_NOPE = 64
QK_ROPE = 32
V_HEAD = 64
KV_LORA = 256
Q_LORA = 768

RET_HEADS = 4
FFN_CONV_W = 3

LANES = 128
SUBLANES = 8
VMEM_LIMIT = 56 * 1024 * 1024

FFN_TS = 512
FFN_FC = 256
LRU_TS = 512
MLA_TS = 512
ATT_TQ = 1024
ATT_TK = 1024
RET_TB = 256
PROJ_TS = 512


def _resident(shape):
    nd = len(shape)
    return pl.BlockSpec(shape, lambda *_: (0,) * nd, pipeline_mode=pl.Buffered(1))


def _rows(ts, width):
    return pl.BlockSpec((ts, width), lambda i: (i, 0))


def _params(n_axes=1):
    return pltpu.CompilerParams(dimension_semantics=("arbitrary",) * n_axes,
                                vmem_limit_bytes=VMEM_LIMIT)


def _dot(a, b):
    return jnp.dot(a, b, preferred_element_type=F32)


def _dot_nt(a, b):
    return lax.dot_general(a, b, (((1,), (1,)), ((), ())), preferred_element_type=F32)


def _dot_tn(a, b):
    return lax.dot_general(a, b, (((0,), (0,)), ((), ())), preferred_element_type=F32)


def _half_gelu2(hx):
    c = math.sqrt(2.0 / math.pi)
    return hx + hx * jnp.tanh(hx * ((8.0 * 0.044715 * c) * (hx * hx) + 2.0 * c))


def _sigmoid(x):
    return 1.0 / (1.0 + jnp.exp(-x))


def _deepnorm_ln(x, mix, g, b):
    y = DEEPNORM_ALPHA * x + mix
    mu = jnp.mean(y, axis=-1, keepdims=True)
    d = y - mu
    var = jnp.mean(d * d, axis=-1, keepdims=True)
    return d * lax.rsqrt(var + LN_EPS) * g + b


def _shifted_tail_tiles(h, tail, sub, n_back, ts):
    tiles = []
    for j in range(n_back, 0, -1):
        cur = h[ts - j * SUBLANES:ts - (j - 1) * SUBLANES, :]
        prev = tail[(j - 1) * SUBLANES:j * SUBLANES, :]
        tiles.append(jnp.where(sub == 0, pltpu.roll(prev, 1, 0), pltpu.roll(cur, 1, 0)))
        tail[(j - 1) * SUBLANES:j * SUBLANES, :] = cur
    return tiles


def _ffn_kernel(x_ref, wg_ref, wu_ref, cw_ref, cb_ref, wd_ref, g_ref, b_ref, o_ref,
                tail_g, tail_u, work_g, work_u, gated_ref, *, ts, fc, n_chunks, f):
    seg = ts // SUBLANES
    d = x_ref.shape[1]
    pad = (FFN_CONV_W - 1) * SUBLANES

    @pl.when(pl.program_id(0) == 0)
    def _():
        tail_g[...] = jnp.zeros_like(tail_g)
        tail_u[...] = jnp.zeros_like(tail_u)

    xb = jnp.swapaxes(x_ref[...].reshape(SUBLANES, seg, d), 0, 1).reshape(ts, d).astype(BF16)
    sub = lax.broadcasted_iota(jnp.int32, (SUBLANES, fc), 0)

    def up_proj(c):
        lo, hi = c * fc, (c + 1) * fc
        for w_ref, tail, work in ((wg_ref, tail_g, work_g), (wu_ref, tail_u, work_u)):
            h = _dot(xb, w_ref[:, lo:hi])
            work[c % 2, pad:pad + ts, :] = h
            for j, tile in enumerate(_shifted_tail_tiles(h, tail.at[c], sub, FFN_CONV_W - 1, ts)):
                work[c % 2, j * SUBLANES:(j + 1) * SUBLANES, :] = tile

    def conv_gate(c):
        lo, hi = c * fc, (c + 1) * fc
        conv = []
        for work, off, scale in ((work_g, 0, 0.5), (work_u, f, 1.0)):
            y = cb_ref[:, off + lo:off + hi] * scale
            for k in range(FFN_CONV_W):
                y = y + (cw_ref[k:k + 1, off + lo:off + hi] * scale) * work[c % 2, k * SUBLANES:k * SUBLANES + ts, :]
            conv.append(y)
        gated_ref[:, lo:hi] = (_half_gelu2(conv[0]) * conv[1]).astype(BF16)

    up_proj(0)
    split = (n_chunks // 2 + 1) * fc
    for c in range(n_chunks):
        if c + 1 < n_chunks:
            up_proj(c + 1)
        conv_gate(c)
        if (c + 1) * fc == split:
            ff_head = _dot(gated_ref[:, :split], wd_ref[:split, :])
    ff = ff_head + _dot(gated_ref[:, split:], wd_ref[split:, :])
    ff = jnp.swapaxes(ff.reshape(seg, SUBLANES, d), 0, 1).reshape(ts, d)
    o_ref[...] = _deepnorm_ln(x_ref[...], ff, g_ref[...], b_ref[...])


def _ffn_layer(x, w_up, conv_w, conv_b, w_down, ln_g, ln_b):
    s, d = x.shape
    f = w_down.shape[0]
    ts, fc = min(FFN_TS, s), FFN_FC
    n_chunks = f // fc
    assert f % fc == 0 and s % ts == 0 and n_chunks >= 2 and ts // SUBLANES >= FFN_CONV_W
    wg = w_up[:, :f].astype(BF16)
    wu = w_up[:, f:].astype(BF16)
    pad = (FFN_CONV_W - 1) * SUBLANES
    kern = functools.partial(_ffn_kernel, ts=ts, fc=fc, n_chunks=n_chunks, f=f)
    return pl.pallas_call(
        kern,
        out_shape=jax.ShapeDtypeStruct((s, d), F32),
        grid=(s // ts,),
        in_specs=[_rows(ts, d), _resident((d, f)), _resident((d, f)),
                  _resident((FFN_CONV_W, 2 * f)), _resident((1, 2 * f)),
                  _resident((f, d)), _resident((1, d)), _resident((1, d))],
        out_specs=_rows(ts, d),
        scratch_shapes=[pltpu.VMEM((n_chunks, pad, fc), F32),
                        pltpu.VMEM((n_chunks, pad, fc), F32),
                        pltpu.VMEM((2, pad + ts, fc), F32),
                        pltpu.VMEM((2, pad + ts, fc), F32),
                        pltpu.VMEM((ts, f), BF16)],
        compiler_params=_params(),
        name="ffn_ln",
    )(x, wg, wu, conv_w, conv_b.reshape(1, -1), w_down.astype(BF16),
      ln_g.reshape(1, -1), ln_b.reshape(1, -1))


def _lru_kernel(x_ref, perm_ref, unperm_ref, wgate_ref, wrnn_ref, cw_ref, cb_ref, wa_ref, ba_ref, wx_ref,
                bx_ref, lam_ref, wo_ref, g_ref, b_ref, o_ref,
                ubuf, tail, a_buf, b_buf, hcarry, *, ts, w):
    nt = ts // SUBLANES
    pad = (LRU_CONV_W - 1) * SUBLANES

    @pl.when(pl.program_id(0) == 0)
    def _():
        tail[...] = jnp.zeros_like(tail)
        hcarry[...] = jnp.zeros_like(hcarry)

    xb = x_ref[...].astype(BF16)
    xp = _dot(perm_ref[...], xb).astype(BF16)
    half_gate = _dot(xp, wgate_ref[...])
    rnn_in = _dot(xp, wrnn_ref[...])

    sub = lax.broadcasted_iota(jnp.int32, (SUBLANES, w), 0)
    ubuf[pad:pad + ts, :] = rnn_in
    for j, tile in enumerate(_shifted_tail_tiles(rnn_in, tail, sub, LRU_CONV_W - 1, ts)):
        ubuf[j * SUBLANES:(j + 1) * SUBLANES, :] = tile
    u = cb_ref[...]
    for k in range(LRU_CONV_W):
        u = u + cw_ref[k:k + 1, :] * ubuf[k * SUBLANES:k * SUBLANES + ts, :]

    ub = u.astype(BF16)
    bw = w // LRU_BLOCKS
    th_r = jnp.tanh(jnp.concatenate([_dot(ub[:, g * bw:(g + 1) * bw], wa_ref[g]) for g in range(LRU_BLOCKS)],
                                    axis=1) + 0.5 * ba_ref[...])
    th_i = jnp.tanh(jnp.concatenate([_dot(ub[:, g * bw:(g + 1) * bw], wx_ref[g]) for g in range(LRU_BLOCKS)],
                                    axis=1) + 0.5 * bx_ref[...])

    neg_lam = -lam_ref[...]
    softplus = jnp.maximum(neg_lam, 0.0) + jnp.log(1.0 + jnp.exp(-jnp.abs(neg_lam)))
    k = (-0.5 * LRU_C) * softplus
    a = jnp.exp(k + k * th_r)
    hu = 0.5 * u
    a_buf[...] = a
    b_buf[...] = jnp.sqrt(1.0 - a * a) * (hu + hu * th_i)

    def seg_step(i, carry):
        hl, pr = carry
        r0 = pl.multiple_of(i * SUBLANES, SUBLANES)
        a_t = a_buf[pl.ds(r0, SUBLANES), :]
        hl = a_t * hl + b_buf[pl.ds(r0, SUBLANES), :]
        pr = a_t * pr
        b_buf[pl.ds(r0, SUBLANES), :] = hl
        a_buf[pl.ds(r0, SUBLANES), :] = pr
        return hl, pr

    e, q = lax.fori_loop(0, nt, seg_step, (jnp.zeros((SUBLANES, w), F32), jnp.ones((SUBLANES, w), F32)),
                         unroll=4)
    for d in (1, 2, 4):
        keep = sub >= d
        e = q * jnp.where(keep, pltpu.roll(e, d, 0), 0.0) + e
        q = q * jnp.where(keep, pltpu.roll(q, d, 0), 1.0)
    h0 = hcarry[...]
    seg_end = e + q * h0
    seg_start = jnp.where(sub == 0, h0, pltpu.roll(seg_end, 1, 0))
    hcarry[...] = jnp.broadcast_to(seg_end[SUBLANES - 1:SUBLANES, :], (SUBLANES, w))
    h = b_buf[...].reshape(nt, SUBLANES, w) + a_buf[...].reshape(nt, SUBLANES, w) * seg_start[None]

    y = (_half_gelu2(half_gate) * h.reshape(ts, w)).astype(BF16)
    y = _dot(unperm_ref[...], y).astype(BF16)
    o_ref[...] = _deepnorm_ln(x_ref[...], _dot(y, wo_ref[...]), g_ref[...], b_ref[...])


def _segment_perm(ts):
    r = jnp.arange(ts)
    src = (r % SUBLANES) * (ts // SUBLANES) + r // SUBLANES
    perm = (src[:, None] == jnp.arange(ts)[None, :]).astype(BF16)
    return perm, perm.T


def _lru_layer(x, w_in, conv_w, conv_b, w_a, b_a, w_x, b_x, lam, w_out, ln_g, ln_b):
    s, d = x.shape
    w = w_out.shape[0]
    ts = min(LRU_TS, s)
    assert s % ts == 0 and ts // SUBLANES >= LRU_CONV_W
    bw = w // LRU_BLOCKS
    pad = (LRU_CONV_W - 1) * SUBLANES
    perm, unperm = _segment_perm(ts)
    kern = functools.partial(_lru_kernel, ts=ts, w=w)
    return pl.pallas_call(
        kern,
        out_shape=jax.ShapeDtypeStruct((s, d), F32),
        grid=(s // ts,),
        in_specs=[_rows(ts, d), _resident((ts, ts)), _resident((ts, ts)), _resident((d, w)), _resident((d, w)),
                  _resident((LRU_CONV_W, w)), _resident((1, w)),
                  _resident((LRU_BLOCKS, bw, bw)), _resident((1, w)),
                  _resident((LRU_BLOCKS, bw, bw)), _resident((1, w)),
                  _resident((1, w)), _resident((w, d)), _resident((1, d)), _resident((1, d))],
        out_specs=_rows(ts, d),
        scratch_shapes=[pltpu.VMEM((pad + ts, w), F32), pltpu.VMEM((pad, w), F32),
                        pltpu.VMEM((ts, w), F32), pltpu.VMEM((ts, w), F32),
                        pltpu.VMEM((SUBLANES, w), F32)],
        compiler_params=_params(),
        name="rglru_ln",
    )(x, perm, unperm, (0.5 * w_in[:, :w]).astype(BF16), w_in[:, w:].astype(BF16), conv_w,
      conv_b.reshape(1, -1), (0.5 * w_a).astype(BF16), b_a.reshape(1, -1), (0.5 * w_x).astype(BF16),
      b_x.reshape(1, -1), lam.reshape(1, -1), w_out.astype(BF16), ln_g.reshape(1, -1), ln_b.reshape(1, -1))


HEAD_PAD = LANES
ROPE_LO = QK_NOPE
ROPE_HALF = QK_ROPE // 2


def _rope_group(v, cos, sin_up, sin_dn):
    return (v * cos + pltpu.roll(v, ROPE_HALF, 1) * sin_up
            + pltpu.roll(v, HEAD_PAD - ROPE_HALF, 1) * sin_dn)


def _mla_proj_kernel(x_ref, pos_ref, invf_ref, wq_ref, wkv_ref, wpe_ref, qg_ref, kvg_ref,
                     wuq_ref, wuk_ref, wuv_ref, vone_ref, q_ref, k_ref, v_ref, *, ts, q_scale):
    xb = x_ref[...].astype(BF16)
    c_q = _dot(xb, wq_ref[...])
    c_kv = _dot(xb, wkv_ref[...])
    k_pe = _dot(xb, wpe_ref[...])

    def rms(v, g):
        return (v * lax.rsqrt(jnp.mean(v * v, axis=-1, keepdims=True) + RMS_EPS) * g).astype(BF16)

    cqn = rms(c_q, qg_ref[...])
    ckvn = rms(c_kv, kvg_ref[...])

    ang = pos_ref[...] * invf_ref[...]
    cos = jnp.cos(ang)
    sin = jnp.sin(ang)
    lane = lax.broadcasted_iota(jnp.int32, (ts, HEAD_PAD), 1)
    sin_up = jnp.where((lane >= ROPE_LO + ROPE_HALF) & (lane < ROPE_LO + QK_ROPE), sin, 0.0)
    sin_dn = jnp.where((lane >= ROPE_LO) & (lane < ROPE_LO + ROPE_HALF), -sin, 0.0)

    k_pe_r = _rope_group(k_pe, cos, sin_up, sin_dn)
    cos_q, sup_q, sdn_q = cos * q_scale, sin_up * q_scale, sin_dn * q_scale
    for h in range(MLA_HEADS):
        lo, hi = h * HEAD_PAD, (h + 1) * HEAD_PAD
        qh = _dot(cqn, wuq_ref[:, lo:hi])
        q_ref[:, lo:hi] = _rope_group(qh, cos_q, sup_q, sdn_q).astype(BF16)
        kh = _dot(ckvn, wuk_ref[:, lo:hi])
        k_ref[:, lo:hi] = (kh + k_pe_r).astype(BF16)
    v_ref[...] = (_dot(ckvn, wuv_ref[...]) + vone_ref[...]).astype(BF16)


def _mla_proj(x, pos, w_in, q_norm_g, kv_norm_g, w_uq, w_ukv):
    s, d = x.shape
    ts = min(MLA_TS, s)
    assert s % ts == 0
    hp = MLA_HEADS * HEAD_PAD
    qk = QK_NOPE + QK_ROPE
    pad = HEAD_PAD - qk
    w_q = w_in[:, :Q_LORA].astype(BF16)
    w_kv = w_in[:, Q_LORA:Q_LORA + KV_LORA].astype(BF16)
    w_pe = jnp.pad(w_in[:, Q_LORA + KV_LORA:], ((0, 0), (ROPE_LO, pad))).astype(BF16)
    w_uq_p = jnp.pad(w_uq.reshape(Q_LORA, MLA_HEADS, qk), ((0, 0), (0, 0), (0, pad)))
    w_uq_p = w_uq_p.reshape(Q_LORA, hp).astype(BF16)
    w_ukv3 = w_ukv.reshape(KV_LORA, MLA_HEADS, QK_NOPE + V_HEAD)
    w_uk_p = jnp.pad(w_ukv3[:, :, :QK_NOPE], ((0, 0), (0, 0), (0, HEAD_PAD - QK_NOPE)))
    w_uk_p = w_uk_p.reshape(KV_LORA, hp).astype(BF16)
    w_uv3 = w_ukv3[:, :, QK_NOPE:].reshape(KV_LORA, MLA_HEADS // 2, 2, V_HEAD)
    zero = jnp.zeros_like(w_uv3[:, :, 0])
    w_uv_p = jnp.stack([w_uv3[:, :, 0], zero, zero, w_uv3[:, :, 1]], axis=2).reshape(KV_LORA, hp).astype(BF16)
    pair_ones = jnp.zeros((2 * HEAD_PAD,), F32).at[V_HEAD].set(1.0).at[HEAD_PAD].set(1.0)
    v_ones = jnp.tile(pair_ones, MLA_HEADS // 2).reshape(1, hp)
    inv_freq = ROPE_BASE ** (-jnp.arange(0, QK_ROPE, 2, dtype=F32) / QK_ROPE)
    invf = jnp.concatenate([jnp.zeros((ROPE_LO,), F32), inv_freq, inv_freq, jnp.zeros((pad,), F32)])
    q_scale = qk ** -0.5 * math.log2(math.e)
    kern = functools.partial(_mla_proj_kernel, ts=ts, q_scale=q_scale)
    return pl.pallas_call(
        kern,
        out_shape=(jax.ShapeDtypeStruct((s, hp), BF16), jax.ShapeDtypeStruct((s, hp), BF16),
                   jax.ShapeDtypeStruct((s, hp), BF16)),
        grid=(s // ts,),
        in_specs=[_rows(ts, d), _rows(ts, 1), _resident((1, HEAD_PAD)),
                  _resident((d, Q_LORA)), _resident((d, KV_LORA)), _resident((d, HEAD_PAD)),
                  _resident((1, Q_LORA)), _resident((1, KV_LORA)),
                  _resident((Q_LORA, hp)), _resident((KV_LORA, hp)), _resident((KV_LORA, hp)),
                  _resident((1, hp))],
        out_specs=(_rows(ts, hp), _rows(ts, hp), _rows(ts, hp)),
        compiler_params=_params(),
        name="mla_proj",
    )(x, pos, invf.reshape(1, -1), w_q, w_kv, w_pe, q_norm_g.reshape(1, -1), kv_norm_g.reshape(1, -1),
      w_uq_p, w_uk_p, w_uv_p, v_ones)


HEADS_PER_STEP = 2


def _attn_kernel(q_ref, k_ref, v_ref, o_ref, m_ref, acc_ref, *, tq, tk):
    i = pl.program_id(1)
    m_ref[...] = jnp.full(m_ref.shape, -jnp.inf, F32)
    acc_ref[...] = jnp.zeros_like(acc_ref)
    reps = tk // LANES

    def tile(j, diag_off):
        r0 = pl.multiple_of(j * tk, tk)
        for h in range(HEADS_PER_STEP):
            lo, hi = h * HEAD_PAD, (h + 1) * HEAD_PAD
            s = _dot_nt(q_ref[:, lo:hi], k_ref[pl.ds(r0, tk), lo:hi])
            if diag_off is not None:
                qc = lax.broadcasted_iota(jnp.int32, (tq, tk), 0) // CHUNK
                kc = (lax.broadcasted_iota(jnp.int32, (tq, tk), 1) + diag_off) // CHUNK
                s = jnp.where(kc <= qc, s, -jnp.inf)
            m_old = m_ref[h]
            m_new = jnp.maximum(m_old, jnp.max(s, axis=-1, keepdims=True))
            alpha = jnp.exp2(m_old - m_new)
            p = jnp.exp2(s - jnp.concatenate([m_new] * reps, axis=1))
            acc_ref[h] = alpha * acc_ref[h] + _dot(p.astype(BF16), v_ref[pl.ds(r0, tk), lo:hi])
            m_ref[h] = m_new

    def body(j, carry):
        tile(j, None)
        return carry

    n_diag = tq // tk
    lax.fori_loop(0, i * n_diag, body, 0)
    for d in range(n_diag):
        tile(i * n_diag + d, d * tk)

    lane = lax.broadcasted_iota(jnp.int32, (tq, LANES), 1)
    a0 = acc_ref[0]
    a1 = acc_ref[1]
    o0 = a0 / a0[:, V_HEAD:V_HEAD + 1]
    o1 = a1 / a1[:, 0:1]
    o_ref[...] = jnp.where(lane < V_HEAD, o0, o1).astype(BF16)


def _mla_attention(q, k, v):
    s = q.shape[0]
    tq = min(ATT_TQ, s)
    tk = min(ATT_TK, tq)
    assert s % tq == 0 and tq % tk == 0 and tk % CHUNK == 0 and HEADS_PER_STEP * V_HEAD == LANES
    n_pairs = MLA_HEADS // HEADS_PER_STEP
    qw = HEADS_PER_STEP * HEAD_PAD
    kern = functools.partial(_attn_kernel, tq=tq, tk=tk)
    return pl.pallas_call(
        kern,
        out_shape=jax.ShapeDtypeStruct((s, MLA_HEADS * V_HEAD), BF16),
        grid=(n_pairs, s // tq),
        in_specs=[pl.BlockSpec((tq, qw), lambda p, i: (i, p)),
                  pl.BlockSpec((s, qw), lambda p, i: (0, p)),
                  pl.BlockSpec((s, qw), lambda p, i: (0, p))],
        out_specs=pl.BlockSpec((tq, LANES), lambda p, i: (i, p)),
        scratch_shapes=[pltpu.VMEM((HEADS_PER_STEP, tq, LANES), F32),
                        pltpu.VMEM((HEADS_PER_STEP, tq, LANES), F32)],
        compiler_params=_params(2),
        name="mla_attention",
    )(q, k, v)


def _proj_ln_kernel(x_ref, y_ref, w_ref, g_ref, b_ref, o_ref):
    o_ref[...] = _deepnorm_ln(x_ref[...], _dot(y_ref[...], w_ref[...]), g_ref[...], b_ref[...])


def _proj_ln(x, y, w, ln_g, ln_b):
    s, d = x.shape
    kdim = y.shape[1]
    ts = min(PROJ_TS, s)
    assert s % ts == 0
    return pl.pallas_call(
        _proj_ln_kernel,
        out_shape=jax.ShapeDtypeStruct((s, d), F32),
        grid=(s // ts,),
        in_specs=[_rows(ts, d), _rows(ts, kdim), _resident((kdim, d)), _resident((1, d)), _resident((1, d))],
        out_specs=_rows(ts, d),
        compiler_params=_params(),
        name="proj_ln",
    )(x, y, w.astype(BF16), ln_g.reshape(1, -1), ln_b.reshape(1, -1))


def _ret_kernel(x_ref, pos_ref, invf_ref, wq_ref, wk_ref, wv_ref, wg_ref, dmat_ref, xi_ref, zeta_ref,
                gn_ref, wo_ref, g_ref, b_ref, o_ref, state_ref, gated_ref,
                *, tb, dqk, dv, k_scale, block_decay):
    @pl.when(pl.program_id(0) == 0)
    def _():
        state_ref[...] = jnp.zeros_like(state_ref)

    x = x_ref[...]
    xb = x.astype(BF16)
    half = dqk // 2
    ang = pos_ref[...] * invf_ref[...]
    cos = jnp.cos(ang)
    sin = jnp.sin(ang)

    def rope(v):
        v1, v2 = v[:, :half], v[:, half:]
        return jnp.concatenate([v1 * cos - v2 * sin, v1 * sin + v2 * cos], axis=1)

    for h in range(RET_HEADS):
        qh = rope(_dot(xb, wq_ref[:, h * dqk:(h + 1) * dqk]))
        kh = rope(_dot(xb, wk_ref[:, h * dqk:(h + 1) * dqk])) * k_scale
        vh = _dot(xb, wv_ref[:, h * dv:(h + 1) * dv]).astype(BF16)
        gh = _dot(xb, wg_ref[:, h * dv:(h + 1) * dv])
        qb = qh.astype(BF16)
        sc = _dot_nt(qb, kh.astype(BF16)) * dmat_ref[h]
        state = state_ref[h]
        y = _dot(sc.astype(BF16), vh) + _dot(qb, state.astype(BF16)) * xi_ref[h]
        state_ref[h] = state * block_decay[h] + _dot_tn((kh * zeta_ref[h]).astype(BF16), vh)
        mu = jnp.mean(y, axis=-1, keepdims=True)
        dlt = y - mu
        var = jnp.mean(dlt * dlt, axis=-1, keepdims=True)
        yn = dlt * lax.rsqrt(var + LN_EPS) * gn_ref[:, h * dv:(h + 1) * dv]
        gated_ref[:, h * dv:(h + 1) * dv] = (gh * _sigmoid(gh) * yn).astype(BF16)
    o_ref[...] = _deepnorm_ln(x, _dot(gated_ref[...], wo_ref[...]), g_ref[...], b_ref[...])


def _ret_layer(x, pos, w_in, gn_g, w_out, ln_g, ln_b):
    s, d = x.shape
    hv = w_out.shape[0]
    dv = hv // RET_HEADS
    hq = (w_in.shape[1] - 2 * hv) // 2
    dqk = hq // RET_HEADS
    tb = min(RET_TB, s)
    assert s % tb == 0 and tb % CHUNK == 0
    half = dqk // 2
    inv_freq = ROPE_BASE ** (-jnp.arange(0, dqk, 2, dtype=F32) / dqk)
    log_gamma = jnp.log1p(-jnp.exp2(-5.0 - jnp.arange(RET_HEADS, dtype=F32)))
    idx = jnp.arange(tb, dtype=F32)
    rel = jnp.abs(idx[:, None] - idx[None, :])
    ic = jnp.arange(tb) // CHUNK
    visible = ic[None, :] <= ic[:, None]
    dmat = jnp.where(visible[None], jnp.exp(log_gamma[:, None, None] * rel), 0.0)
    xi = jnp.exp(log_gamma[:, None] * (idx + 1.0)[None, :])[:, :, None]
    zeta = jnp.exp(log_gamma[:, None] * (tb - 1.0 - idx)[None, :])[:, :, None]
    block_decay = tuple(float(math.exp(math.log1p(-2.0 ** (-5.0 - h)) * tb)) for h in range(RET_HEADS))
    kern = functools.partial(_ret_kernel, tb=tb, dqk=dqk, dv=dv, k_scale=dqk ** -0.5,
                             block_decay=block_decay)
    return pl.pallas_call(
        kern,
        out_shape=jax.ShapeDtypeStruct((s, d), F32),
        grid=(s // tb,),
        in_specs=[_rows(tb, d), _rows(tb, 1), _resident((1, half)),
                  _resident((d, hq)), _resident((d, hq)), _resident((d, hv)), _resident((d, hv)),
                  _resident((RET_HEADS, tb, tb)), _resident((RET_HEADS, tb, 1)), _resident((RET_HEADS, tb, 1)),
                  _resident((1, hv)), _resident((hv, d)), _resident((1, d)), _resident((1, d))],
        out_specs=_rows(tb, d),
        scratch_shapes=[pltpu.VMEM((RET_HEADS, dqk, dv), F32), pltpu.VMEM((tb, hv), BF16)],
        compiler_params=_params(),
        name="retention_ln",
    )(x, pos, inv_freq.reshape(1, -1),
      w_in[:, :hq].astype(BF16), w_in[:, hq:2 * hq].astype(BF16),
      w_in[:, 2 * hq:2 * hq + hv].astype(BF16), w_in[:, 2 * hq + hv:].astype(BF16),
      dmat, xi, zeta, gn_g.reshape(1, -1), w_out.astype(BF16), ln_g.reshape(1, -1), ln_b.reshape(1, -1))


def kernel(x, positions, ln1_g, ln1_b, ln2_g, ln2_b, ffn_w_up, ffn_conv_w, ffn_conv_b, ffn_w_down,
           lru_w_in, lru_conv_w, lru_conv_b, lru_w_a, lru_b_a, lru_w_x, lru_b_x, lru_lambda, lru_w_out,
           mla_w_in, mla_q_norm, mla_kv_norm, mla_w_uq, mla_w_ukv, mla_w_out,
           ret_w_in, ret_gn_g, ret_w_out):
    bsz, s, d = x.shape
    outs = []
    for bi in range(bsz):
        h = x[bi]
        pos = positions[bi].astype(F32).reshape(s, 1)
        for i in range(DEPTH):
            kind, j = i % N_MIXERS, i // N_MIXERS
            if kind == 0:
                h = _lru_layer(h, lru_w_in[j], lru_conv_w[j], lru_conv_b[j], lru_w_a[j], lru_b_a[j],
                               lru_w_x[j], lru_b_x[j], lru_lambda[j], lru_w_out[j], ln1_g[i], ln1_b[i])
            elif kind == 1:
                q, k, v = _mla_proj(h, pos, mla_w_in[j], mla_q_norm[j], mla_kv_norm[j],
                                    mla_w_uq[j], mla_w_ukv[j])
                o = _mla_attention(q, k, v)
                h = _proj_ln(h, o, mla_w_out[j], ln1_g[i], ln1_b[i])
            else:
                h = _ret_layer(h, pos, ret_w_in[j], ret_gn_g[j], ret_w_out[j], ln1_g[i], ln1_b[i])
            h = _ffn_layer(h, ffn_w_up[i], ffn_conv_w[i], ffn_conv_b[i], ffn_w_down[i], ln2_g[i], ln2_b[i])
        outs.append(h)
    return jnp.stack(outs, axis=0)
```

```python
import functools
import math

import jax
import jax.numpy as jnp
from jax import lax
from jax.experimental import pallas as pl
from jax.experimental.pallas import tpu as pltpu

F32 = jnp.float32
BF16 = jnp.bfloat16

DEPTH = 4
N_MIXERS = 3
CHUNK = 64
DEEPNORM_ALPHA = (2.0 * DEPTH) ** 0.25
LN_EPS = 1e-5
RMS_EPS = 1e-6
ROPE_BASE = 10000.0

LRU_BLOCKS = 4
LRU_CONV_W = 4
LRU_C = 8.0

MLA_HEADS = 16
QK_NOPE = 64
QK_ROPE = 32
V_HEAD = 64
KV_LORA = 256
Q_LORA = 768

RET_HEADS = 4
FFN_CONV_W = 3

LANES = 128
SUBLANES = 8
VMEM_LIMIT = 56 * 1024 * 1024

FFN_TS = 512
FFN_FC = 256
LRU_TS = 512
MLA_TS = 512
ATT_TQ = 1024
ATT_TK = 1024
RET_TB = 256
PROJ_TS = 1024


def _resident(shape):
    nd = len(shape)
    return pl.BlockSpec(shape, lambda *_: (0,) * nd, pipeline_mode=pl.Buffered(1))


def _rows(ts, width):
    return pl.BlockSpec((ts, width), lambda i: (i, 0))


def _params(n_axes=1):
    return pltpu.CompilerParams(dimension_semantics=("arbitrary",) * n_axes,
                                vmem_limit_bytes=VMEM_LIMIT)


def _dot(a, b):
    return jnp.dot(a, b, preferred_element_type=F32)


def _dot_nt(a, b):
    return lax.dot_general(a, b, (((1,), (1,)), ((), ())), preferred_element_type=F32)


def _dot_tn(a, b):
    return lax.dot_general(a, b, (((0,), (0,)), ((), ())), preferred_element_type=F32)


def _half_gelu2(hx):
    c = math.sqrt(2.0 / math.pi)
    return hx + hx * jnp.tanh(hx * ((8.0 * 0.044715 * c) * (hx * hx) + 2.0 * c))


def _sigmoid(x):
    return 1.0 / (1.0 + jnp.exp(-x))


def _deepnorm_ln(x, mix, g, b):
    y = DEEPNORM_ALPHA * x + mix
    mu = jnp.mean(y, axis=-1, keepdims=True)
    d = y - mu
    var = jnp.mean(d * d, axis=-1, keepdims=True)
    return d * lax.rsqrt(var + LN_EPS) * g + b


def _shifted_tail_tiles(h, tail, sub, n_back, ts):
    tiles = []
    for j in range(n_back, 0, -1):
        cur = h[ts - j * SUBLANES:ts - (j - 1) * SUBLANES, :]
        prev = tail[(j - 1) * SUBLANES:j * SUBLANES, :]
        tiles.append(jnp.where(sub == 0, pltpu.roll(prev, 1, 0), pltpu.roll(cur, 1, 0)))
        tail[(j - 1) * SUBLANES:j * SUBLANES, :] = cur
    return tiles


def _ffn_kernel(x_ref, wg_ref, wu_ref, cw_ref, cb_ref, wd_ref, g_ref, b_ref, o_ref,
                tail_g, tail_u, work_g, work_u, gated_ref, *, ts, fc, n_chunks, f):
    seg = ts // SUBLANES
    d = x_ref.shape[1]
    pad = (FFN_CONV_W - 1) * SUBLANES

    @pl.when(pl.program_id(0) == 0)
    def _():
        tail_g[...] = jnp.zeros_like(tail_g)
        tail_u[...] = jnp.zeros_like(tail_u)

    xb = jnp.swapaxes(x_ref[...].reshape(SUBLANES, seg, d), 0, 1).reshape(ts, d).astype(BF16)
    sub = lax.broadcasted_iota(jnp.int32, (SUBLANES, fc), 0)

    def up_proj(c):
        lo, hi = c * fc, (c + 1) * fc
        for w_ref, tail, work in ((wg_ref, tail_g, work_g), (wu_ref, tail_u, work_u)):
            h = _dot(xb, w_ref[:, lo:hi])
            work[c % 2, pad:pad + ts, :] = h
            for j, tile in enumerate(_shifted_tail_tiles(h, tail.at[c], sub, FFN_CONV_W - 1, ts)):
                work[c % 2, j * SUBLANES:(j + 1) * SUBLANES, :] = tile

    def conv_gate(c):
        lo, hi = c * fc, (c + 1) * fc
        conv = []
        for work, off, scale in ((work_g, 0, 0.5), (work_u, f, 1.0)):
            y = cb_ref[:, off + lo:off + hi] * scale
            for k in range(FFN_CONV_W):
                y = y + (cw_ref[k:k + 1, off + lo:off + hi] * scale) * work[c % 2, k * SUBLANES:k * SUBLANES + ts, :]
            conv.append(y)
        gated_ref[:, lo:hi] = (_half_gelu2(conv[0]) * conv[1]).astype(BF16)

    up_proj(0)
    split = (n_chunks // 2 + 1) * fc
    for c in range(n_chunks):
        if c + 1 < n_chunks:
            up_proj(c + 1)
        conv_gate(c)
        if (c + 1) * fc == split:
            ff_head = _dot(gated_ref[:, :split], wd_ref[:split, :])
    ff = ff_head + _dot(gated_ref[:, split:], wd_ref[split:, :])
    ff = jnp.swapaxes(ff.reshape(seg, SUBLANES, d), 0, 1).reshape(ts, d)
    o_ref[...] = _deepnorm_ln(x_ref[...], ff, g_ref[...], b_ref[...])


def _ffn_layer(x, w_up, conv_w, conv_b, w_down, ln_g, ln_b):
    s, d = x.shape
    f = w_down.shape[0]
    ts, fc = min(FFN_TS, s), FFN_FC
    n_chunks = f // fc
    assert f % fc == 0 and s % ts == 0 and n_chunks >= 2 and ts // SUBLANES >= FFN_CONV_W
    wg = w_up[:, :f].astype(BF16)
    wu = w_up[:, f:].astype(BF16)
    pad = (FFN_CONV_W - 1) * SUBLANES
    kern = functools.partial(_ffn_kernel, ts=ts, fc=fc, n_chunks=n_chunks, f=f)
    return pl.pallas_call(
        kern,
        out_shape=jax.ShapeDtypeStruct((s, d), F32),
        grid=(s // ts,),
        in_specs=[_rows(ts, d), _resident((d, f)), _resident((d, f)),
                  _resident((FFN_CONV_W, 2 * f)), _resident((1, 2 * f)),
                  _resident((f, d)), _resident((1, d)), _resident((1, d))],
        out_specs=_rows(ts, d),
        scratch_shapes=[pltpu.VMEM((n_chunks, pad, fc), F32),
                        pltpu.VMEM((n_chunks, pad, fc), F32),
                        pltpu.VMEM((2, pad + ts, fc), F32),
                        pltpu.VMEM((2, pad + ts, fc), F32),
                        pltpu.VMEM((ts, f), BF16)],
        compiler_params=_params(),
        name="ffn_ln",
    )(x, wg, wu, conv_w, conv_b.reshape(1, -1), w_down.astype(BF16),
      ln_g.reshape(1, -1), ln_b.reshape(1, -1))


def _lru_kernel(x_ref, perm_ref, unperm_ref, wgate_ref, wrnn_ref, cw_ref, cb_ref, wa_ref, ba_ref, wx_ref,
                bx_ref, lam_ref, wo_ref, g_ref, b_ref, o_ref,
                ubuf, tail, a_buf, b_buf, hcarry, *, ts, w):
    nt = ts // SUBLANES
    pad = (LRU_CONV_W - 1) * SUBLANES

    @pl.when(pl.program_id(0) == 0)
    def _():
        tail[...] = jnp.zeros_like(tail)
        hcarry[...] = jnp.zeros_like(hcarry)

    xb = x_ref[...].astype(BF16)
    xp = _dot(perm_ref[...], xb).astype(BF16)
    half_gate = _dot(xp, wgate_ref[...])
    rnn_in = _dot(xp, wrnn_ref[...])

    sub = lax.broadcasted_iota(jnp.int32, (SUBLANES, w), 0)
    ubuf[pad:pad + ts, :] = rnn_in
    for j, tile in enumerate(_shifted_tail_tiles(rnn_in, tail, sub, LRU_CONV_W - 1, ts)):
        ubuf[j * SUBLANES:(j + 1) * SUBLANES, :] = tile
    u = cb_ref[...]
    for k in range(LRU_CONV_W):
        u = u + cw_ref[k:k + 1, :] * ubuf[k * SUBLANES:k * SUBLANES + ts, :]

    ub = u.astype(BF16)
    bw = w // LRU_BLOCKS
    th_r = jnp.tanh(jnp.concatenate([_dot(ub[:, g * bw:(g + 1) * bw], wa_ref[g]) for g in range(LRU_BLOCKS)],
                                    axis=1) + 0.5 * ba_ref[...])
    th_i = jnp.tanh(jnp.concatenate([_dot(ub[:, g * bw:(g + 1) * bw], wx_ref[g]) for g in range(LRU_BLOCKS)],
                                    axis=1) + 0.5 * bx_ref[...])

    neg_lam = -lam_ref[...]
    softplus = jnp.maximum(neg_lam, 0.0) + jnp.log(1.0 + jnp.exp(-jnp.abs(neg_lam)))
    k = (-0.5 * LRU_C) * softplus
    a = jnp.exp(k + k * th_r)
    hu = 0.5 * u
    a_buf[...] = a
    b_buf[...] = jnp.sqrt(1.0 - a * a) * (hu + hu * th_i)

    def seg_step(i, carry):
        hl, pr = carry
        r0 = pl.multiple_of(i * SUBLANES, SUBLANES)
        a_t = a_buf[pl.ds(r0, SUBLANES), :]
        hl = a_t * hl + b_buf[pl.ds(r0, SUBLANES), :]
        pr = a_t * pr
        b_buf[pl.ds(r0, SUBLANES), :] = hl
        a_buf[pl.ds(r0, SUBLANES), :] = pr
        return hl, pr

    e, q = lax.fori_loop(0, nt, seg_step, (jnp.zeros((SUBLANES, w), F32), jnp.ones((SUBLANES, w), F32)),
                         unroll=4)
    for d in (1, 2, 4):
        keep = sub >= d
        e = q * jnp.where(keep, pltpu.roll(e, d, 0), 0.0) + e
        q = q * jnp.where(keep, pltpu.roll(q, d, 0), 1.0)
    h0 = hcarry[...]
    seg_end = e + q * h0
    seg_start = jnp.where(sub == 0, h0, pltpu.roll(seg_end, 1, 0))
    hcarry[...] = jnp.broadcast_to(seg_end[SUBLANES - 1:SUBLANES, :], (SUBLANES, w))
    h = b_buf[...].reshape(nt, SUBLANES, w) + a_buf[...].reshape(nt, SUBLANES, w) * seg_start[None]

    y = (_half_gelu2(half_gate) * h.reshape(ts, w)).astype(BF16)
    y = _dot(unperm_ref[...], y).astype(BF16)
    o_ref[...] = _deepnorm_ln(x_ref[...], _dot(y, wo_ref[...]), g_ref[...], b_ref[...])


def _segment_perm(ts):
    r = jnp.arange(ts)
    src = (r % SUBLANES) * (ts // SUBLANES) + r // SUBLANES
    perm = (src[:, None] == jnp.arange(ts)[None, :]).astype(BF16)
    return perm, perm.T


def _lru_layer(x, w_in, conv_w, conv_b, w_a, b_a, w_x, b_x, lam, w_out, ln_g, ln_b):
    s, d = x.shape
    w = w_out.shape[0]
    ts = min(LRU_TS, s)
    assert s % ts == 0 and ts // SUBLANES >= LRU_CONV_W
    bw = w // LRU_BLOCKS
    pad = (LRU_CONV_W - 1) * SUBLANES
    perm, unperm = _segment_perm(ts)
    kern = functools.partial(_lru_kernel, ts=ts, w=w)
    return pl.pallas_call(
        kern,
        out_shape=jax.ShapeDtypeStruct((s, d), F32),
        grid=(s // ts,),
        in_specs=[_rows(ts, d), _resident((ts, ts)), _resident((ts, ts)), _resident((d, w)), _resident((d, w)),
                  _resident((LRU_CONV_W, w)), _resident((1, w)),
                  _resident((LRU_BLOCKS, bw, bw)), _resident((1, w)),
                  _resident((LRU_BLOCKS, bw, bw)), _resident((1, w)),
                  _resident((1, w)), _resident((w, d)), _resident((1, d)), _resident((1, d))],
        out_specs=_rows(ts, d),
        scratch_shapes=[pltpu.VMEM((pad + ts, w), F32), pltpu.VMEM((pad, w), F32),
                        pltpu.VMEM((ts, w), F32), pltpu.VMEM((ts, w), F32),
                        pltpu.VMEM((SUBLANES, w), F32)],
        compiler_params=_params(),
        name="rglru_ln",
    )(x, perm, unperm, (0.5 * w_in[:, :w]).astype(BF16), w_in[:, w:].astype(BF16), conv_w,
      conv_b.reshape(1, -1), (0.5 * w_a).astype(BF16), b_a.reshape(1, -1), (0.5 * w_x).astype(BF16),
      b_x.reshape(1, -1), lam.reshape(1, -1), w_out.astype(BF16), ln_g.reshape(1, -1), ln_b.reshape(1, -1))


HEAD_PAD = LANES
ROPE_LO = QK_NOPE
ROPE_HALF = QK_ROPE // 2


def _rope_group(v, cos, sin_up, sin_dn):
    width = v.shape[1]
    return (v * cos + pltpu.roll(v, ROPE_HALF, 1) * sin_up
            + pltpu.roll(v, width - ROPE_HALF, 1) * sin_dn)


def _mla_proj_kernel(x_ref, pos_ref, invf_ref, wq_ref, wkv_ref, wpe_ref, qg_ref, kvg_ref,
                     wuq_ref, wuk_ref, wuv_ref, vone_ref, q_ref, k_ref, v_ref, *, ts, q_scale):
    xb = x_ref[...].astype(BF16)
    c_q = _dot(xb, wq_ref[...])
    c_kv = _dot(xb, wkv_ref[...])
    k_pe = _dot(xb, wpe_ref[...])

    def rms(v, g):
        return (v * lax.rsqrt(jnp.mean(v * v, axis=-1, keepdims=True) + RMS_EPS) * g).astype(BF16)

    cqn = rms(c_q, qg_ref[...])
    ckvn = rms(c_kv, kvg_ref[...])

    ang = pos_ref[...] * invf_ref[...]
    cos = jnp.cos(ang)
    sin = jnp.sin(ang)
    lane = lax.broadcasted_iota(jnp.int32, (ts, HEAD_PAD), 1)
    sin_up = jnp.where((lane >= ROPE_LO + ROPE_HALF) & (lane < ROPE_LO + QK_ROPE), sin, 0.0)
    sin_dn = jnp.where((lane >= ROPE_LO) & (lane < ROPE_LO + ROPE_HALF), -sin, 0.0)

    k_pe_r = _rope_group(k_pe, cos, sin_up, sin_dn)
    pair = lambda t: jnp.concatenate([t, t], axis=1)
    cos_q, sup_q, sdn_q = pair(cos * q_scale), pair(sin_up * q_scale), pair(sin_dn * q_scale)
    k_pe_r2 = pair(k_pe_r)
    for h in range(0, MLA_HEADS, 2):
        lo, hi = h * HEAD_PAD, (h + 2) * HEAD_PAD
        q2 = _dot(cqn, wuq_ref[:, lo:hi])
        q_ref[:, lo:hi] = _rope_group(q2, cos_q, sup_q, sdn_q).astype(BF16)
        k2 = _dot(ckvn, wuk_ref[:, lo:hi])
        k_ref[:, lo:hi] = (k2 + k_pe_r2).astype(BF16)
    v_ref[...] = (_dot(ckvn, wuv_ref[...]) + vone_ref[...]).astype(BF16)


def _mla_proj(x, pos, w_in, q_norm_g, kv_norm_g, w_uq, w_ukv):
    s, d = x.shape
    ts = min(MLA_TS, s)
    assert s % ts == 0
    hp = MLA_HEADS * HEAD_PAD
    qk = QK_NOPE + QK_ROPE
    pad = HEAD_PAD - qk
    w_q = w_in[:, :Q_LORA].astype(BF16)
    w_kv = w_in[:, Q_LORA:Q_LORA + KV_LORA].astype(BF16)
    w_pe = jnp.pad(w_in[:, Q_LORA + KV_LORA:], ((0, 0), (ROPE_LO, pad))).astype(BF16)
    w_uq_p = jnp.pad(w_uq.reshape(Q_LORA, MLA_HEADS, qk), ((0, 0), (0, 0), (0, pad)))
    w_uq_p = w_uq_p.reshape(Q_LORA, hp).astype(BF16)
    w_ukv3 = w_ukv.reshape(KV_LORA, MLA_HEADS, QK_NOPE + V_HEAD)
    w_uk_p = jnp.pad(w_ukv3[:, :, :QK_NOPE], ((0, 0), (0, 0), (0, HEAD_PAD - QK_NOPE)))
    w_uk_p = w_uk_p.reshape(KV_LORA, hp).astype(BF16)
    w_uv3 = w_ukv3[:, :, QK_NOPE:].reshape(KV_LORA, MLA_HEADS // 2, 2, V_HEAD)
    zero = jnp.zeros_like(w_uv3[:, :, 0])
    w_uv_p = jnp.stack([w_uv3[:, :, 0], zero, zero, w_uv3[:, :, 1]], axis=2).reshape(KV_LORA, hp).astype(BF16)
    pair_ones = jnp.zeros((2 * HEAD_PAD,), F32).at[V_HEAD].set(1.0).at[HEAD_PAD].set(1.0)
    v_ones = jnp.tile(pair_ones, MLA_HEADS // 2).reshape(1, hp)
    inv_freq = ROPE_BASE ** (-jnp.arange(0, QK_ROPE, 2, dtype=F32) / QK_ROPE)
    invf = jnp.concatenate([jnp.zeros((ROPE_LO,), F32), inv_freq, inv_freq, jnp.zeros((pad,), F32)])
    q_scale = qk ** -0.5 * math.log2(math.e)
    kern = functools.partial(_mla_proj_kernel, ts=ts, q_scale=q_scale)
    return pl.pallas_call(
        kern,
        out_shape=(jax.ShapeDtypeStruct((s, hp), BF16), jax.ShapeDtypeStruct((s, hp), BF16),
                   jax.ShapeDtypeStruct((s, hp), BF16)),
        grid=(s // ts,),
        in_specs=[_rows(ts, d), _rows(ts, 1), _resident((1, HEAD_PAD)),
                  _resident((d, Q_LORA)), _resident((d, KV_LORA)), _resident((d, HEAD_PAD)),
                  _resident((1, Q_LORA)), _resident((1, KV_LORA)),
                  _resident((Q_LORA, hp)), _resident((KV_LORA, hp)), _resident((KV_LORA, hp)),
                  _resident((1, hp))],
        out_specs=(_rows(ts, hp), _rows(ts, hp), _rows(ts, hp)),
        compiler_params=_params(),
        name="mla_proj",
    )(x, pos, invf.reshape(1, -1), w_q, w_kv, w_pe, q_norm_g.reshape(1, -1), kv_norm_g.reshape(1, -1),
      w_uq_p, w_uk_p, w_uv_p, v_ones)


HEADS_PER_STEP = 2


def _attn_kernel(q_ref, k_ref, v_ref, o_ref, m_ref, acc_ref, *, tq, tk):
    i = pl.program_id(1)
    m_ref[...] = jnp.full(m_ref.shape, -jnp.inf, F32)
    acc_ref[...] = jnp.zeros_like(acc_ref)
    reps = tk // LANES

    def tile(j, diag_off):
        r0 = pl.multiple_of(j * tk, tk)
        for h in range(HEADS_PER_STEP):
            lo, hi = h * HEAD_PAD, (h + 1) * HEAD_PAD
            s = _dot_nt(q_ref[:, lo:hi], k_ref[pl.ds(r0, tk), lo:hi])
            if diag_off is not None:
                qc = lax.broadcasted_iota(jnp.int32, (tq, tk), 0) // CHUNK
                kc = (lax.broadcasted_iota(jnp.int32, (tq, tk), 1) + diag_off) // CHUNK
                s = jnp.where(kc <= qc, s, -jnp.inf)
            m_old = m_ref[h]
            m_new = jnp.maximum(m_old, jnp.max(s, axis=-1, keepdims=True))
            alpha = jnp.exp2(m_old - m_new)
            p = jnp.exp2(s - jnp.concatenate([m_new] * reps, axis=1))
            acc_ref[h] = alpha * acc_ref[h] + _dot(p.astype(BF16), v_ref[pl.ds(r0, tk), lo:hi])
            m_ref[h] = m_new

    def body(j, carry):
        tile(2 * j, None)
        tile(2 * j + 1, None)
        return carry

    n_diag = tq // tk
    n_full = i * n_diag
    lax.fori_loop(0, n_full // 2, body, 0)

    @pl.when(n_full % 2 == 1)
    def _():
        tile(n_full - 1, None)

    for d in range(n_diag):
        tile(i * n_diag + d, d * tk)

    lane = lax.broadcasted_iota(jnp.int32, (tq, LANES), 1)
    a0 = acc_ref[0]
    a1 = acc_ref[1]
    o0 = a0 / a0[:, V_HEAD:V_HEAD + 1]
    o1 = a1 / a1[:, 0:1]
    o_ref[...] = jnp.where(lane < V_HEAD, o0, o1).astype(BF16)


def _mla_attention(q, k, v):
    s = q.shape[0]
    tq = min(ATT_TQ, s)
    tk = min(ATT_TK, tq)
    assert s % tq == 0 and tq % tk == 0 and tk % CHUNK == 0 and HEADS_PER_STEP * V_HEAD == LANES
    n_pairs = MLA_HEADS // HEADS_PER_STEP
    qw = HEADS_PER_STEP * HEAD_PAD
    kern = functools.partial(_attn_kernel, tq=tq, tk=tk)
    return pl.pallas_call(
        kern,
        out_shape=jax.ShapeDtypeStruct((s, MLA_HEADS * V_HEAD), BF16),
        grid=(n_pairs, s // tq),
        in_specs=[pl.BlockSpec((tq, qw), lambda p, i: (i, p)),
                  pl.BlockSpec((s, qw), lambda p, i: (0, p)),
                  pl.BlockSpec((s, qw), lambda p, i: (0, p))],
        out_specs=pl.BlockSpec((tq, LANES), lambda p, i: (i, p)),
        scratch_shapes=[pltpu.VMEM((HEADS_PER_STEP, tq, LANES), F32),
                        pltpu.VMEM((HEADS_PER_STEP, tq, LANES), F32)],
        compiler_params=_params(2),
        name="mla_attention",
    )(q, k, v)


def _proj_ln_kernel(x_ref, y_ref, w_ref, g_ref, b_ref, o_ref):
    o_ref[...] = _deepnorm_ln(x_ref[...], _dot(y_ref[...], w_ref[...]), g_ref[...], b_ref[...])


def _proj_ln(x, y, w, ln_g, ln_b):
    s, d = x.shape
    kdim = y.shape[1]
    ts = min(PROJ_TS, s)
    assert s % ts == 0
    return pl.pallas_call(
        _proj_ln_kernel,
        out_shape=jax.ShapeDtypeStruct((s, d), F32),
        grid=(s // ts,),
        in_specs=[_rows(ts, d), _rows(ts, kdim), _resident((kdim, d)), _resident((1, d)), _resident((1, d))],
        out_specs=_rows(ts, d),
        compiler_params=_params(),
        name="proj_ln",
    )(x, y, w.astype(BF16), ln_g.reshape(1, -1), ln_b.reshape(1, -1))


def _ret_kernel(x_ref, pos_ref, invf_ref, wq_ref, wk_ref, wv_ref, wg_ref, dmat_ref, xi_ref, zeta_ref,
                gn_ref, wo_ref, g_ref, b_ref, o_ref, state_ref, gated_ref,
                *, tb, dqk, dv, k_scale, block_decay):
    @pl.when(pl.program_id(0) == 0)
    def _():
        state_ref[...] = jnp.zeros_like(state_ref)

    x = x_ref[...]
    xb = x.astype(BF16)
    half = dqk // 2
    ang = pos_ref[...] * invf_ref[...]
    cos = jnp.cos(ang)
    sin = jnp.sin(ang)

    def rope(v):
        v1, v2 = v[:, :half], v[:, half:]
        return jnp.concatenate([v1 * cos - v2 * sin, v1 * sin + v2 * cos], axis=1)

    for h in range(RET_HEADS):
        qh = rope(_dot(xb, wq_ref[:, h * dqk:(h + 1) * dqk]))
        kh = rope(_dot(xb, wk_ref[:, h * dqk:(h + 1) * dqk])) * k_scale
        vh = _dot(xb, wv_ref[:, h * dv:(h + 1) * dv]).astype(BF16)
        gh = _dot(xb, wg_ref[:, h * dv:(h + 1) * dv])
        qb = qh.astype(BF16)
        sc = _dot_nt(qb, kh.astype(BF16)) * dmat_ref[h]
        state = state_ref[h]
        y = _dot(sc.astype(BF16), vh) + _dot(qb, state.astype(BF16)) * xi_ref[h]
        state_ref[h] = state * block_decay[h] + _dot_tn((kh * zeta_ref[h]).astype(BF16), vh)
        mu = jnp.mean(y, axis=-1, keepdims=True)
        dlt = y - mu
        var = jnp.mean(dlt * dlt, axis=-1, keepdims=True)
        yn = dlt * lax.rsqrt(var + LN_EPS) * gn_ref[:, h * dv:(h + 1) * dv]
        gated_ref[:, h * dv:(h + 1) * dv] = (gh * _sigmoid(gh) * yn).astype(BF16)
    o_ref[...] = _deepnorm_ln(x, _dot(gated_ref[...], wo_ref[...]), g_ref[...], b_ref[...])


def _ret_layer(x, pos, w_in, gn_g, w_out, ln_g, ln_b):
    s, d = x.shape
    hv = w_out.shape[0]
    dv = hv // RET_HEADS
    hq = (w_in.shape[1] - 2 * hv) // 2
    dqk = hq // RET_HEADS
    tb = min(RET_TB, s)
    assert s % tb == 0 and tb % CHUNK == 0
    half = dqk // 2
    inv_freq = ROPE_BASE ** (-jnp.arange(0, dqk, 2, dtype=F32) / dqk)
    log_gamma = jnp.log1p(-jnp.exp2(-5.0 - jnp.arange(RET_HEADS, dtype=F32)))
    idx = jnp.arange(tb, dtype=F32)
    rel = jnp.abs(idx[:, None] - idx[None, :])
    ic = jnp.arange(tb) // CHUNK
    visible = ic[None, :] <= ic[:, None]
    dmat = jnp.where(visible[None], jnp.exp(log_gamma[:, None, None] * rel), 0.0)
    xi = jnp.exp(log_gamma[:, None] * (idx + 1.0)[None, :])[:, :, None]
    zeta = jnp.exp(log_gamma[:, None] * (tb - 1.0 - idx)[None, :])[:, :, None]
    block_decay = tuple(float(math.exp(math.log1p(-2.0 ** (-5.0 - h)) * tb)) for h in range(RET_HEADS))
    kern = functools.partial(_ret_kernel, tb=tb, dqk=dqk, dv=dv, k_scale=dqk ** -0.5,
                             block_decay=block_decay)
    return pl.pallas_call(
        kern,
        out_shape=jax.ShapeDtypeStruct((s, d), F32),
        grid=(s // tb,),
        in_specs=[_rows(tb, d), _rows(tb, 1), _resident((1, half)),
                  _resident((d, hq)), _resident((d, hq)), _resident((d, hv)), _resident((d, hv)),
                  _resident((RET_HEADS, tb, tb)), _resident((RET_HEADS, tb, 1)), _resident((RET_HEADS, tb, 1)),
                  _resident((1, hv)), _resident((hv, d)), _resident((1, d)), _resident((1, d))],
        out_specs=_rows(tb, d),
        scratch_shapes=[pltpu.VMEM((RET_HEADS, dqk, dv), F32), pltpu.VMEM((tb, hv), BF16)],
        compiler_params=_params(),
        name="retention_ln",
    )(x, pos, inv_freq.reshape(1, -1),
      w_in[:, :hq].astype(BF16), w_in[:, hq:2 * hq].astype(BF16),
      w_in[:, 2 * hq:2 * hq + hv].astype(BF16), w_in[:, 2 * hq + hv:].astype(BF16),
      dmat, xi, zeta, gn_g.reshape(1, -1), w_out.astype(BF16), ln_g.reshape(1, -1), ln_b.reshape(1, -1))


def kernel(x, positions, ln1_g, ln1_b, ln2_g, ln2_b, ffn_w_up, ffn_conv_w, ffn_conv_b, ffn_w_down,
           lru_w_in, lru_conv_w, lru_conv_b, lru_w_a, lru_b_a, lru_w_x, lru_b_x, lru_lambda, lru_w_out,
           mla_w_in, mla_q_norm, mla_kv_norm, mla_w_uq, mla_w_ukv, mla_w_out,
           ret_w_in, ret_gn_g, ret_w_out):
    bsz, s, d = x.shape
    outs = []
    for bi in range(bsz):
        h = x[bi]
        pos = positions[bi].astype(F32).reshape(s, 1)
        for i in range(DEPTH):
            kind, j = i % N_MIXERS, i // N_MIXERS
            if kind == 0:
                h = _lru_layer(h, lru_w_in[j], lru_conv_w[j], lru_conv_b[j], lru_w_a[j], lru_b_a[j],
                               lru_w_x[j], lru_b_x[j], lru_lambda[j], lru_w_out[j], ln1_g[i], ln1_b[i])
            elif kind == 1:
                q, k, v = _mla_proj(h, pos, mla_w_in[j], mla_q_norm[j], mla_kv_norm[j],
                                    mla_w_uq[j], mla_w_ukv[j])
                o = _mla_attention(q, k, v)
                h = _proj_ln(h, o, mla_w_out[j], ln1_g[i], ln1_b[i])
            else:
                h = _ret_layer(h, pos, ret_w_in[j], ret_gn_g[j], ret_w_out[j], ln1_g[i], ln1_b[i])
            h = _ffn_layer(h, ffn_w_up[i], ffn_conv_w[i], ffn_conv_b[i], ffn_w_down[i], ln2_g[i], ln2_b[i])
        outs.append(h)
    return outs[0][None] if bsz == 1 else jnp.stack(outs, axis=0)
```

```python
import functools
import math

import jax
import jax.numpy as jnp
from jax import lax
from jax.experimental import pallas as pl
from jax.experimental.pallas import tpu as pltpu

F32 = jnp.float32
BF16 = jnp.bfloat16

DEPTH = 4
N_MIXERS = 3
CHUNK = 64
DEEPNORM_ALPHA = (2.0 * DEPTH) ** 0.25
LN_EPS = 1e-5
RMS_EPS = 1e-6
ROPE_BASE = 10000.0

LRU_BLOCKS = 4
LRU_CONV_W = 4
LRU_C = 8.0

MLA_HEADS = 16
QK_NOPE = 64
QK_ROPE = 32
V_HEAD = 64
KV_LORA = 256
Q_LORA = 768

RET_HEADS = 4
FFN_CONV_W = 3

LANES = 128
SUBLANES = 8
VMEM_LIMIT = 56 * 1024 * 1024

FFN_TS = 512
FFN_FC = 256
LRU_TS = 512
MLA_TS = 512
ATT_TQ = 1024
ATT_TK = 1024
RET_TB = 256
PROJ_TS = 1024


def _resident(shape):
    nd = len(shape)
    return pl.BlockSpec(shape, lambda *_: (0,) * nd, pipeline_mode=pl.Buffered(1))


def _rows(ts, width):
    return pl.BlockSpec((ts, width), lambda i: (i, 0))


def _params(n_axes=1):
    return pltpu.CompilerParams(dimension_semantics=("arbitrary",) * n_axes,
                                vmem_limit_bytes=VMEM_LIMIT)


def _dot(a, b):
    return jnp.dot(a, b, preferred_element_type=F32)


def _dot_nt(a, b):
    return lax.dot_general(a, b, (((1,), (1,)), ((), ())), preferred_element_type=F32)


def _dot_tn(a, b):
    return lax.dot_general(a, b, (((0,), (0,)), ((), ())), preferred_element_type=F32)


def _half_gelu2(hx):
    c = math.sqrt(2.0 / math.pi)
    return hx + hx * jnp.tanh(hx * ((8.0 * 0.044715 * c) * (hx * hx) + 2.0 * c))


def _sigmoid(x):
    return 1.0 / (1.0 + jnp.exp(-x))


def _deepnorm_ln(x, mix, g, b):
    y = DEEPNORM_ALPHA * x + mix
    mu = jnp.mean(y, axis=-1, keepdims=True)
    d = y - mu
    var = jnp.mean(d * d, axis=-1, keepdims=True)
    return d * lax.rsqrt(var + LN_EPS) * g + b


def _to_segments(x):
    rows, d = x.shape
    return jnp.swapaxes(x.reshape(SUBLANES, rows // SUBLANES, d), 0, 1).reshape(rows, d)


def _from_segments(x):
    rows, d = x.shape
    return jnp.swapaxes(x.reshape(rows // SUBLANES, SUBLANES, d), 0, 1).reshape(rows, d)


def _shifted_tail_tiles(h, tail, sub, n_back, ts):
    tiles = []
    for j in range(n_back, 0, -1):
        cur = h[ts - j * SUBLANES:ts - (j - 1) * SUBLANES, :]
        prev = tail[(j - 1) * SUBLANES:j * SUBLANES, :]
        tiles.append(jnp.where(sub == 0, pltpu.roll(prev, 1, 0), pltpu.roll(cur, 1, 0)))
        tail[(j - 1) * SUBLANES:j * SUBLANES, :] = cur
    return tiles


def _ffn_kernel(x_ref, wg_ref, wu_ref, cw_ref, cb_ref, wd_ref, g_ref, b_ref, o_ref,
                tail_g, tail_u, work_g, work_u, gated_ref, *, ts, fc, n_chunks, f):
    pad = (FFN_CONV_W - 1) * SUBLANES

    @pl.when(pl.program_id(0) == 0)
    def _():
        tail_g[...] = jnp.zeros_like(tail_g)
        tail_u[...] = jnp.zeros_like(tail_u)

    xb = _to_segments(x_ref[...]).astype(BF16)
    sub = lax.broadcasted_iota(jnp.int32, (SUBLANES, fc), 0)

    def up_proj(c):
        lo, hi = c * fc, (c + 1) * fc
        for w_ref, tail, work in ((wg_ref, tail_g, work_g), (wu_ref, tail_u, work_u)):
            h = _dot(xb, w_ref[:, lo:hi])
            work[c % 2, pad:pad + ts, :] = h
            for j, tile in enumerate(_shifted_tail_tiles(h, tail.at[c], sub, FFN_CONV_W - 1, ts)):
                work[c % 2, j * SUBLANES:(j + 1) * SUBLANES, :] = tile

    def conv_gate(c):
        lo, hi = c * fc, (c + 1) * fc
        conv = []
        for work, off, scale in ((work_g, 0, 0.5), (work_u, f, 1.0)):
            y = cb_ref[:, off + lo:off + hi] * scale
            for k in range(FFN_CONV_W):
                y = y + (cw_ref[k:k + 1, off + lo:off + hi] * scale) * work[c % 2, k * SUBLANES:k * SUBLANES + ts, :]
            conv.append(y)
        gated_ref[:, lo:hi] = (_half_gelu2(conv[0]) * conv[1]).astype(BF16)

    up_proj(0)
    split = (n_chunks // 2 + 1) * fc
    for c in range(n_chunks):
        if c + 1 < n_chunks:
            up_proj(c + 1)
        conv_gate(c)
        if (c + 1) * fc == split:
            ff_head = _dot(gated_ref[:, :split], wd_ref[:split, :])
    ff = ff_head + _dot(gated_ref[:, split:], wd_ref[split:, :])
    o_ref[...] = _deepnorm_ln(x_ref[...], _from_segments(ff), g_ref[...], b_ref[...])


def _ffn_layer(x, w_up, conv_w, conv_b, w_down, ln_g, ln_b):
    s, d = x.shape
    f = w_down.shape[0]
    ts, fc = min(FFN_TS, s), FFN_FC
    n_chunks = f // fc
    assert f % fc == 0 and s % ts == 0 and n_chunks >= 2 and ts // SUBLANES >= FFN_CONV_W
    wg = w_up[:, :f].astype(BF16)
    wu = w_up[:, f:].astype(BF16)
    pad = (FFN_CONV_W - 1) * SUBLANES
    kern = functools.partial(_ffn_kernel, ts=ts, fc=fc, n_chunks=n_chunks, f=f)
    return pl.pallas_call(
        kern,
        out_shape=jax.ShapeDtypeStruct((s, d), F32),
        grid=(s // ts,),
        in_specs=[_rows(ts, d), _resident((d, f)), _resident((d, f)),
                  _resident((FFN_CONV_W, 2 * f)), _resident((1, 2 * f)),
                  _resident((f, d)), _resident((1, d)), _resident((1, d))],
        out_specs=_rows(ts, d),
        scratch_shapes=[pltpu.VMEM((n_chunks, pad, fc), F32),
                        pltpu.VMEM((n_chunks, pad, fc), F32),
                        pltpu.VMEM((2, pad + ts, fc), F32),
                        pltpu.VMEM((2, pad + ts, fc), F32),
                        pltpu.VMEM((ts, f), BF16)],
        compiler_params=_params(),
        name="ffn_ln",
    )(x, wg, wu, conv_w, conv_b.reshape(1, -1), w_down.astype(BF16),
      ln_g.reshape(1, -1), ln_b.reshape(1, -1))


def _lru_kernel(x_ref, wgate_ref, wrnn_ref, cw_ref, cb_ref, wa_ref, ba_ref, wx_ref,
                bx_ref, lam_ref, wo_ref, g_ref, b_ref, o_ref,
                ubuf, tail, a_buf, b_buf, hcarry, *, ts, w):
    nt = ts // SUBLANES
    pad = (LRU_CONV_W - 1) * SUBLANES

    @pl.when(pl.program_id(0) == 0)
    def _():
        tail[...] = jnp.zeros_like(tail)
        hcarry[...] = jnp.zeros_like(hcarry)

    xp = _to_segments(x_ref[...]).astype(BF16)
    half_gate = _dot(xp, wgate_ref[...])
    rnn_in = _dot(xp, wrnn_ref[...])

    sub = lax.broadcasted_iota(jnp.int32, (SUBLANES, w), 0)
    ubuf[pad:pad + ts, :] = rnn_in
    for j, tile in enumerate(_shifted_tail_tiles(rnn_in, tail, sub, LRU_CONV_W - 1, ts)):
        ubuf[j * SUBLANES:(j + 1) * SUBLANES, :] = tile
    u = cb_ref[...]
    for k in range(LRU_CONV_W):
        u = u + cw_ref[k:k + 1, :] * ubuf[k * SUBLANES:k * SUBLANES + ts, :]

    ub = u.astype(BF16)
    bw = w // LRU_BLOCKS
    th_r = jnp.tanh(jnp.concatenate([_dot(ub[:, g * bw:(g + 1) * bw], wa_ref[g]) for g in range(LRU_BLOCKS)],
                                    axis=1) + 0.5 * ba_ref[...])
    th_i = jnp.tanh(jnp.concatenate([_dot(ub[:, g * bw:(g + 1) * bw], wx_ref[g]) for g in range(LRU_BLOCKS)],
                                    axis=1) + 0.5 * bx_ref[...])

    neg_lam = -lam_ref[...]
    softplus = jnp.maximum(neg_lam, 0.0) + jnp.log(1.0 + jnp.exp(-jnp.abs(neg_lam)))
    k = (-0.5 * LRU_C) * softplus
    a = jnp.exp(k + k * th_r)
    hu = 0.5 * u
    a_buf[...] = a
    b_buf[...] = jnp.sqrt(1.0 - a * a) * (hu + hu * th_i)

    def seg_step(i, carry):
        hl, pr = carry
        r0 = pl.multiple_of(i * SUBLANES, SUBLANES)
        a_t = a_buf[pl.ds(r0, SUBLANES), :]
        hl = a_t * hl + b_buf[pl.ds(r0, SUBLANES), :]
        pr = a_t * pr
        b_buf[pl.ds(r0, SUBLANES), :] = hl
        a_buf[pl.ds(r0, SUBLANES), :] = pr
        return hl, pr

    e, q = lax.fori_loop(0, nt, seg_step, (jnp.zeros((SUBLANES, w), F32), jnp.ones((SUBLANES, w), F32)),
                         unroll=4)
    for d in (1, 2, 4):
        keep = sub >= d
        e = q * jnp.where(keep, pltpu.roll(e, d, 0), 0.0) + e
        q = q * jnp.where(keep, pltpu.roll(q, d, 0), 1.0)
    h0 = hcarry[...]
    seg_end = e + q * h0
    seg_start = jnp.where(sub == 0, h0, pltpu.roll(seg_end, 1, 0))
    hcarry[...] = jnp.broadcast_to(seg_end[SUBLANES - 1:SUBLANES, :], (SUBLANES, w))
    h = b_buf[...].reshape(nt, SUBLANES, w) + a_buf[...].reshape(nt, SUBLANES, w) * seg_start[None]

    y = (_half_gelu2(half_gate) * h.reshape(ts, w)).astype(BF16)
    mix = _from_segments(_dot(y, wo_ref[...]))
    o_ref[...] = _deepnorm_ln(x_ref[...], mix, g_ref[...], b_ref[...])


def _lru_layer(x, w_in, conv_w, conv_b, w_a, b_a, w_x, b_x, lam, w_out, ln_g, ln_b):
    s, d = x.shape
    w = w_out.shape[0]
    ts = min(LRU_TS, s)
    assert s % ts == 0 and ts // SUBLANES >= LRU_CONV_W
    bw = w // LRU_BLOCKS
    pad = (LRU_CONV_W - 1) * SUBLANES
    kern = functools.partial(_lru_kernel, ts=ts, w=w)
    return pl.pallas_call(
        kern,
        out_shape=jax.ShapeDtypeStruct((s, d), F32),
        grid=(s // ts,),
        in_specs=[_rows(ts, d), _resident((d, w)), _resident((d, w)),
                  _resident((LRU_CONV_W, w)), _resident((1, w)),
                  _resident((LRU_BLOCKS, bw, bw)), _resident((1, w)),
                  _resident((LRU_BLOCKS, bw, bw)), _resident((1, w)),
                  _resident((1, w)), _resident((w, d)), _resident((1, d)), _resident((1, d))],
        out_specs=_rows(ts, d),
        scratch_shapes=[pltpu.VMEM((pad + ts, w), F32), pltpu.VMEM((pad, w), F32),
                        pltpu.VMEM((ts, w), F32), pltpu.VMEM((ts, w), F32),
                        pltpu.VMEM((SUBLANES, w), F32)],
        compiler_params=_params(),
        name="rglru_ln",
    )(x, (0.5 * w_in[:, :w]).astype(BF16), w_in[:, w:].astype(BF16), conv_w,
      conv_b.reshape(1, -1), (0.5 * w_a).astype(BF16), b_a.reshape(1, -1), (0.5 * w_x).astype(BF16),
      b_x.reshape(1, -1), lam.reshape(1, -1), w_out.astype(BF16), ln_g.reshape(1, -1), ln_b.reshape(1, -1))


HEAD_PAD = LANES
ROPE_LO = QK_NOPE
ROPE_HALF = QK_ROPE // 2


def _rope_group(v, cos, sin_up, sin_dn):
    width = v.shape[1]
    return (v * cos + pltpu.roll(v, ROPE_HALF, 1) * sin_up
            + pltpu.roll(v, width - ROPE_HALF, 1) * sin_dn)


def _mla_proj_kernel(x_ref, pos_ref, invf_ref, wq_ref, wkv_ref, wpe_ref, qg_ref, kvg_ref,
                     wuq_ref, wuk_ref, wuv_ref, vone_ref, q_ref, k_ref, v_ref, *, ts, q_scale):
    xb = x_ref[...].astype(BF16)
    c_q = _dot(xb, wq_ref[...])
    c_kv = _dot(xb, wkv_ref[...])
    k_pe = _dot(xb, wpe_ref[...])

    def rms(v, g):
        return (v * lax.rsqrt(jnp.mean(v * v, axis=-1, keepdims=True) + RMS_EPS) * g).astype(BF16)

    cqn = rms(c_q, qg_ref[...])
    ckvn = rms(c_kv, kvg_ref[...])

    ang = pos_ref[...] * invf_ref[...]
    cos = jnp.cos(ang)
    sin = jnp.sin(ang)
    lane = lax.broadcasted_iota(jnp.int32, (ts, HEAD_PAD), 1)
    sin_up = jnp.where((lane >= ROPE_LO + ROPE_HALF) & (lane < ROPE_LO + QK_ROPE), sin, 0.0)
    sin_dn = jnp.where((lane >= ROPE_LO) & (lane < ROPE_LO + ROPE_HALF), -sin, 0.0)

    k_pe_r = _rope_group(k_pe, cos, sin_up, sin_dn)
    pair = lambda t: jnp.concatenate([t, t], axis=1)
    cos_q, sup_q, sdn_q = pair(cos * q_scale), pair(sin_up * q_scale), pair(sin_dn * q_scale)
    k_pe_r2 = pair(k_pe_r)
    for h in range(0, MLA_HEADS, 2):
        lo, hi = h * HEAD_PAD, (h + 2) * HEAD_PAD
        q2 = _dot(cqn, wuq_ref[:, lo:hi])
        q_ref[:, lo:hi] = _rope_group(q2, cos_q, sup_q, sdn_q).astype(BF16)
        k2 = _dot(ckvn, wuk_ref[:, lo:hi])
        k_ref[:, lo:hi] = (k2 + k_pe_r2).astype(BF16)
    v_ref[...] = (_dot(ckvn, wuv_ref[...]) + vone_ref[...]).astype(BF16)


def _mla_proj(x, pos, w_in, q_norm_g, kv_norm_g, w_uq, w_ukv):
    s, d = x.shape
    ts = min(MLA_TS, s)
    assert s % ts == 0
    hp = MLA_HEADS * HEAD_PAD
    qk = QK_NOPE + QK_ROPE
    pad = HEAD_PAD - qk
    w_q = w_in[:, :Q_LORA].astype(BF16)
    w_kv = w_in[:, Q_LORA:Q_LORA + KV_LORA].astype(BF16)
    w_pe = jnp.pad(w_in[:, Q_LORA + KV_LORA:], ((0, 0), (ROPE_LO, pad))).astype(BF16)
    w_uq_p = jnp.pad(w_uq.reshape(Q_LORA, MLA_HEADS, qk), ((0, 0), (0, 0), (0, pad)))
    w_uq_p = w_uq_p.reshape(Q_LORA, hp).astype(BF16)
    w_ukv3 = w_ukv.reshape(KV_LORA, MLA_HEADS, QK_NOPE + V_HEAD)
    w_uk_p = jnp.pad(w_ukv3[:, :, :QK_NOPE], ((0, 0), (0, 0), (0, HEAD_PAD - QK_NOPE)))
    w_uk_p = w_uk_p.reshape(KV_LORA, hp).astype(BF16)
    w_uv3 = w_ukv3[:, :, QK_NOPE:].reshape(KV_LORA, MLA_HEADS // 2, 2, V_HEAD)
    zero = jnp.zeros_like(w_uv3[:, :, 0])
    w_uv_p = jnp.stack([w_uv3[:, :, 0], zero, zero, w_uv3[:, :, 1]], axis=2).reshape(KV_LORA, hp).astype(BF16)
    pair_ones = jnp.zeros((2 * HEAD_PAD,), F32).at[V_HEAD].set(1.0).at[HEAD_PAD].set(1.0)
    v_ones = jnp.tile(pair_ones, MLA_HEADS // 2).reshape(1, hp)
    inv_freq = ROPE_BASE ** (-jnp.arange(0, QK_ROPE, 2, dtype=F32) / QK_ROPE)
    invf = jnp.concatenate([jnp.zeros((ROPE_LO,), F32), inv_freq, inv_freq, jnp.zeros((pad,), F32)])
    q_scale = qk ** -0.5 * math.log2(math.e)
    kern = functools.partial(_mla_proj_kernel, ts=ts, q_scale=q_scale)
    return pl.pallas_call(
        kern,
        out_shape=(jax.ShapeDtypeStruct((s, hp), BF16), jax.ShapeDtypeStruct((s, hp), BF16),
                   jax.ShapeDtypeStruct((s, hp), BF16)),
        grid=(s // ts,),
        in_specs=[_rows(ts, d), _rows(ts, 1), _resident((1, HEAD_PAD)),
                  _resident((d, Q_LORA)), _resident((d, KV_LORA)), _resident((d, HEAD_PAD)),
                  _resident((1, Q_LORA)), _resident((1, KV_LORA)),
                  _resident((Q_LORA, hp)), _resident((KV_LORA, hp)), _resident((KV_LORA, hp)),
                  _resident((1, hp))],
        out_specs=(_rows(ts, hp), _rows(ts, hp), _rows(ts, hp)),
        compiler_params=_params(),
        name="mla_proj",
    )(x, pos, invf.reshape(1, -1), w_q, w_kv, w_pe, q_norm_g.reshape(1, -1), kv_norm_g.reshape(1, -1),
      w_uq_p, w_uk_p, w_uv_p, v_ones)


HEADS_PER_STEP = 2


def _attn_kernel(q_ref, k_ref, v_ref, o_ref, m_ref, acc_ref, *, tq, tk):
    i = pl.program_id(1)
    m_ref[...] = jnp.full(m_ref.shape, -jnp.inf, F32)
    acc_ref[...] = jnp.zeros_like(acc_ref)
    reps = tk // LANES

    def tile(j, diag_off):
        r0 = pl.multiple_of(j * tk, tk)
        for h in range(HEADS_PER_STEP):
            lo, hi = h * HEAD_PAD, (h + 1) * HEAD_PAD
            s = _dot_nt(q_ref[:, lo:hi], k_ref[pl.ds(r0, tk), lo:hi])
            if diag_off is not None:
                qc = lax.broadcasted_iota(jnp.int32, (tq, tk), 0) // CHUNK
                kc = (lax.broadcasted_iota(jnp.int32, (tq, tk), 1) + diag_off) // CHUNK
                s = jnp.where(kc <= qc, s, -jnp.inf)
            m_old = m_ref[h]
            m_new = jnp.maximum(m_old, jnp.max(s, axis=-1, keepdims=True))
            alpha = jnp.exp2(m_old - m_new)
            p = jnp.exp2(s - jnp.concatenate([m_new] * reps, axis=1))
            acc_ref[h] = alpha * acc_ref[h] + _dot(p.astype(BF16), v_ref[pl.ds(r0, tk), lo:hi])
            m_ref[h] = m_new

    def body(j, carry):
        tile(2 * j, None)
        tile(2 * j + 1, None)
        return carry

    n_diag = tq // tk
    n_full = i * n_diag
    lax.fori_loop(0, n_full // 2, body, 0)

    @pl.when(n_full % 2 == 1)
    def _():
        tile(n_full - 1, None)

    for d in range(n_diag):
        tile(i * n_diag + d, d * tk)

    lane = lax.broadcasted_iota(jnp.int32, (tq, LANES), 1)
    a0 = acc_ref[0]
    a1 = acc_ref[1]
    o0 = a0 / a0[:, V_HEAD:V_HEAD + 1]
    o1 = a1 / a1[:, 0:1]
    o_ref[...] = jnp.where(lane < V_HEAD, o0, o1).astype(BF16)


def _mla_attention(q, k, v):
    s = q.shape[0]
    tq = min(ATT_TQ, s)
    tk = min(ATT_TK, tq)
    assert s % tq == 0 and tq % tk == 0 and tk % CHUNK == 0 and HEADS_PER_STEP * V_HEAD == LANES
    n_pairs = MLA_HEADS // HEADS_PER_STEP
    qw = HEADS_PER_STEP * HEAD_PAD
    kern = functools.partial(_attn_kernel, tq=tq, tk=tk)
    return pl.pallas_call(
        kern,
        out_shape=jax.ShapeDtypeStruct((s, MLA_HEADS * V_HEAD), BF16),
        grid=(n_pairs, s // tq),
        in_specs=[pl.BlockSpec((tq, qw), lambda p, i: (i, p)),
                  pl.BlockSpec((s, qw), lambda p, i: (0, p)),
                  pl.BlockSpec((s, qw), lambda p, i: (0, p))],
        out_specs=pl.BlockSpec((tq, LANES), lambda p, i: (i, p)),
        scratch_shapes=[pltpu.VMEM((HEADS_PER_STEP, tq, LANES), F32),
                        pltpu.VMEM((HEADS_PER_STEP, tq, LANES), F32)],
        compiler_params=_params(2),
        name="mla_attention",
    )(q, k, v)


def _proj_ln_kernel(x_ref, y_ref, w_ref, g_ref, b_ref, o_ref):
    o_ref[...] = _deepnorm_ln(x_ref[...], _dot(y_ref[...], w_ref[...]), g_ref[...], b_ref[...])


def _proj_ln(x, y, w, ln_g, ln_b):
    s, d = x.shape
    kdim = y.shape[1]
    ts = min(PROJ_TS, s)
    assert s % ts == 0
    return pl.pallas_call(
        _proj_ln_kernel,
        out_shape=jax.ShapeDtypeStruct((s, d), F32),
        grid=(s // ts,),
        in_specs=[_rows(ts, d), _rows(ts, kdim), _resident((kdim, d)), _resident((1, d)), _resident((1, d))],
        out_specs=_rows(ts, d),
        compiler_params=_params(),
        name="proj_ln",
    )(x, y, w.astype(BF16), ln_g.reshape(1, -1), ln_b.reshape(1, -1))


def _ret_kernel(x_ref, pos_ref, invf_ref, wq_ref, wk_ref, wv_ref, wg_ref, dmat_ref, xi_ref, zeta_ref,
                gn_ref, wo_ref, g_ref, b_ref, o_ref, state_ref, gated_ref,
                *, tb, dqk, dv, k_scale, block_decay):
    @pl.when(pl.program_id(0) == 0)
    def _():
        state_ref[...] = jnp.zeros_like(state_ref)

    x = x_ref[...]
    xb = x.astype(BF16)
    half = dqk // 2
    ang = pos_ref[...] * invf_ref[...]
    cos = jnp.cos(ang)
    sin = jnp.sin(ang)

    def rope(v):
        v1, v2 = v[:, :half], v[:, half:]
        return jnp.concatenate([v1 * cos - v2 * sin, v1 * sin + v2 * cos], axis=1)

    for h in range(RET_HEADS):
        qh = rope(_dot(xb, wq_ref[:, h * dqk:(h + 1) * dqk]))
        kh = rope(_dot(xb, wk_ref[:, h * dqk:(h + 1) * dqk])) * k_scale
        vh = _dot(xb, wv_ref[:, h * dv:(h + 1) * dv]).astype(BF16)
        gh = _dot(xb, wg_ref[:, h * dv:(h + 1) * dv])
        qb = qh.astype(BF16)
        sc = _dot_nt(qb, kh.astype(BF16)) * dmat_ref[h]
        state = state_ref[h]
        y = _dot(sc.astype(BF16), vh) + _dot(qb, state.astype(BF16)) * xi_ref[h]
        state_ref[h] = state * block_decay[h] + _dot_tn((kh * zeta_ref[h]).astype(BF16), vh)
        mu = jnp.mean(y, axis=-1, keepdims=True)
        dlt = y - mu
        var = jnp.mean(dlt * dlt, axis=-1, keepdims=True)
        yn = dlt * lax.rsqrt(var + LN_EPS) * gn_ref[:, h * dv:(h + 1) * dv]
        gated_ref[:, h * dv:(h + 1) * dv] = (gh * _sigmoid(gh) * yn).astype(BF16)
    o_ref[...] = _deepnorm_ln(x, _dot(gated_ref[...], wo_ref[...]), g_ref[...], b_ref[...])


def _ret_layer(x, pos, w_in, gn_g, w_out, ln_g, ln_b):
    s, d = x.shape
    hv = w_out.shape[0]
    dv = hv // RET_HEADS
    hq = (w_in.shape[1] - 2 * hv) // 2
    dqk = hq // RET_HEADS
    tb = min(RET_TB, s)
    assert s % tb == 0 and tb % CHUNK == 0
    half = dqk // 2
    inv_freq = ROPE_BASE ** (-jnp.arange(0, dqk, 2, dtype=F32) / dqk)
    log_gamma = jnp.log1p(-jnp.exp2(-5.0 - jnp.arange(RET_HEADS, dtype=F32)))
    idx = jnp.arange(tb, dtype=F32)
    rel = jnp.abs(idx[:, None] - idx[None, :])
    ic = jnp.arange(tb) // CHUNK
    visible = ic[None, :] <= ic[:, None]
    dmat = jnp.where(visible[None], jnp.exp(log_gamma[:, None, None] * rel), 0.0)
    xi = jnp.exp(log_gamma[:, None] * (idx + 1.0)[None, :])[:, :, None]
    zeta = jnp.exp(log_gamma[:, None] * (tb - 1.0 - idx)[None, :])[:, :, None]
    block_decay = tuple(float(math.exp(math.log1p(-2.0 ** (-5.0 - h)) * tb)) for h in range(RET_HEADS))
    kern = functools.partial(_ret_kernel, tb=tb, dqk=dqk, dv=dv, k_scale=dqk ** -0.5,
                             block_decay=block_decay)
    return pl.pallas_call(
        kern,
        out_shape=jax.ShapeDtypeStruct((s, d), F32),
        grid=(s // tb,),
        in_specs=[_rows(tb, d), _rows(tb, 1), _resident((1, half)),
                  _resident((d, hq)), _resident((d, hq)), _resident((d, hv)), _resident((d, hv)),
                  _resident((RET_HEADS, tb, tb)), _resident((RET_HEADS, tb, 1)), _resident((RET_HEADS, tb, 1)),
                  _resident((1, hv)), _resident((hv, d)), _resident((1, d)), _resident((1, d))],
        out_specs=_rows(tb, d),
        scratch_shapes=[pltpu.VMEM((RET_HEADS, dqk, dv), F32), pltpu.VMEM((tb, hv), BF16)],
        compiler_params=_params(),
        name="retention_ln",
    )(x, pos, inv_freq.reshape(1, -1),
      w_in[:, :hq].astype(BF16), w_in[:, hq:2 * hq].astype(BF16),
      w_in[:, 2 * hq:2 * hq + hv].astype(BF16), w_in[:, 2 * hq + hv:].astype(BF16),
      dmat, xi, zeta, gn_g.reshape(1, -1), w_out.astype(BF16), ln_g.reshape(1, -1), ln_b.reshape(1, -1))


def kernel(x, positions, ln1_g, ln1_b, ln2_g, ln2_b, ffn_w_up, ffn_conv_w, ffn_conv_b, ffn_w_down,
           lru_w_in, lru_conv_w, lru_conv_b, lru_w_a, lru_b_a, lru_w_x, lru_b_x, lru_lambda, lru_w_out,
           mla_w_in, mla_q_norm, mla_kv_norm, mla_w_uq, mla_w_ukv, mla_w_out,
           ret_w_in, ret_gn_g, ret_w_out):
    bsz, s, d = x.shape
    outs = []
    for bi in range(bsz):
        h = x[bi]
        pos = positions[bi].astype(F32).reshape(s, 1)
        for i in range(DEPTH):
            kind, j = i % N_MIXERS, i // N_MIXERS
            if kind == 0:
                h = _lru_layer(h, lru_w_in[j], lru_conv_w[j], lru_conv_b[j], lru_w_a[j], lru_b_a[j],
                               lru_w_x[j], lru_b_x[j], lru_lambda[j], lru_w_out[j], ln1_g[i], ln1_b[i])
            elif kind == 1:
                q, k, v = _mla_proj(h, pos, mla_w_in[j], mla_q_norm[j], mla_kv_norm[j],
                                    mla_w_uq[j], mla_w_ukv[j])
                o = _mla_attention(q, k, v)
                h = _proj_ln(h, o, mla_w_out[j], ln1_g[i], ln1_b[i])
            else:
                h = _ret_layer(h, pos, ret_w_in[j], ret_gn_g[j], ret_w_out[j], ln1_g[i], ln1_b[i])
            h = _ffn_layer(h, ffn_w_up[i], ffn_conv_w[i], ffn_conv_b[i], ffn_w_down[i], ln2_g[i], ln2_b[i])
        outs.append(h)
    return outs[0][None] if bsz == 1 else jnp.stack(outs, axis=0)
```

```python
import functools
import math

import jax
import jax.numpy as jnp
from jax import lax
from jax.experimental import pallas as pl
from jax.experimental.pallas import tpu as pltpu

F32 = jnp.float32
BF16 = jnp.bfloat16

DEPTH = 4
N_MIXERS = 3
CHUNK = 64
DEEPNORM_ALPHA = (2.0 * DEPTH) ** 0.25
LN_EPS = 1e-5
RMS_EPS = 1e-6
ROPE_BASE = 10000.0

LRU_BLOCKS = 4
LRU_CONV_W = 4
LRU_C = 8.0

MLA_HEADS = 16
QK_NOPE = 64
QK_ROPE = 32
V_HEAD = 64
KV_LORA = 256
Q_LORA = 768

RET_HEADS = 4
FFN_CONV_W = 3

LANES = 128
SUBLANES = 8
VMEM_LIMIT = 56 * 1024 * 1024

FFN_TS = 512
FFN_FC = 256
LRU_TS = 512
MLA_TS = 512
ATT_TQ = 1024
ATT_TK = 1024
RET_TB = 256
PROJ_TS = 1024


def _resident(shape):
    nd = len(shape)
    return pl.BlockSpec(shape, lambda *_: (0,) * nd, pipeline_mode=pl.Buffered(1))


def _rows(ts, width):
    return pl.BlockSpec((ts, width), lambda i: (i, 0))


def _params(n_axes=1):
    return pltpu.CompilerParams(dimension_semantics=("arbitrary",) * n_axes,
                                vmem_limit_bytes=VMEM_LIMIT)


def _dot(a, b):
    return jnp.dot(a, b, preferred_element_type=F32)


def _dot_nt(a, b):
    return lax.dot_general(a, b, (((1,), (1,)), ((), ())), preferred_element_type=F32)


def _dot_tn(a, b):
    return lax.dot_general(a, b, (((0,), (0,)), ((), ())), preferred_element_type=F32)


def _half_gelu2(hx):
    c = math.sqrt(2.0 / math.pi)
    return hx + hx * jnp.tanh(hx * ((8.0 * 0.044715 * c) * (hx * hx) + 2.0 * c))


def _sigmoid(x):
    return 1.0 / (1.0 + jnp.exp(-x))


def _deepnorm_ln(x, mix, g, b):
    y = DEEPNORM_ALPHA * x + mix
    mu = jnp.mean(y, axis=-1, keepdims=True)
    d = y - mu
    var = jnp.mean(d * d, axis=-1, keepdims=True)
    return d * lax.rsqrt(var + LN_EPS) * g + b


def _to_segments(x):
    rows, d = x.shape
    return jnp.swapaxes(x.reshape(SUBLANES, rows // SUBLANES, d), 0, 1).reshape(rows, d)


def _segment_ln_tail(x_ref, o_ref, parts, g, b):
    ts, d = x_ref.shape
    seg_len = ts // SUBLANES
    run = seg_len // len(parts)
    for r, part in enumerate(parts):
        nat = jnp.swapaxes(part.reshape(run, SUBLANES, d), 0, 1).reshape(SUBLANES * run, d)
        rows = [slice(s * seg_len + r * run, s * seg_len + (r + 1) * run) for s in range(SUBLANES)]
        out = _deepnorm_ln(jnp.concatenate([x_ref[rs, :] for rs in rows], axis=0), nat, g, b)
        for s, rs in enumerate(rows):
            o_ref[rs, :] = out[s * run:(s + 1) * run, :]


def _shifted_tail_tiles(h, tail, sub, n_back, ts):
    tiles = []
    for j in range(n_back, 0, -1):
        cur = h[ts - j * SUBLANES:ts - (j - 1) * SUBLANES, :]
        prev = tail[(j - 1) * SUBLANES:j * SUBLANES, :]
        tiles.append(jnp.where(sub == 0, pltpu.roll(prev, 1, 0), pltpu.roll(cur, 1, 0)))
        tail[(j - 1) * SUBLANES:j * SUBLANES, :] = cur
    return tiles


def _ffn_kernel(x_ref, wg_ref, wu_ref, cw_ref, cb_ref, wd_ref, g_ref, b_ref, o_ref,
                tail_g, tail_u, work_g, work_u, gated_ref, *, ts, fc, n_chunks, f):
    pad = (FFN_CONV_W - 1) * SUBLANES

    @pl.when(pl.program_id(0) == 0)
    def _():
        tail_g[...] = jnp.zeros_like(tail_g)
        tail_u[...] = jnp.zeros_like(tail_u)

    xb = _to_segments(x_ref[...]).astype(BF16)
    sub = lax.broadcasted_iota(jnp.int32, (SUBLANES, fc), 0)

    def up_proj(c):
        lo, hi = c * fc, (c + 1) * fc
        for w_ref, tail, work in ((wg_ref, tail_g, work_g), (wu_ref, tail_u, work_u)):
            h = _dot(xb, w_ref[:, lo:hi])
            work[c % 2, pad:pad + ts, :] = h
            for j, tile in enumerate(_shifted_tail_tiles(h, tail.at[c], sub, FFN_CONV_W - 1, ts)):
                work[c % 2, j * SUBLANES:(j + 1) * SUBLANES, :] = tile

    def conv_gate(c):
        lo, hi = c * fc, (c + 1) * fc
        conv = []
        for work, off, scale in ((work_g, 0, 0.5), (work_u, f, 1.0)):
            y = cb_ref[:, off + lo:off + hi] * scale
            for k in range(FFN_CONV_W):
                y = y + (cw_ref[k:k + 1, off + lo:off + hi] * scale) * work[c % 2, k * SUBLANES:k * SUBLANES + ts, :]
            conv.append(y)
        gated_ref[:, lo:hi] = (_half_gelu2(conv[0]) * conv[1]).astype(BF16)

    up_proj(0)
    split = (n_chunks // 2 + 1) * fc
    for c in range(n_chunks):
        if c + 1 < n_chunks:
            up_proj(c + 1)
        conv_gate(c)
        if (c + 1) * fc == split:
            ff_head = _dot(gated_ref[:, :split], wd_ref[:split, :])
    half = ts // 2
    parts = [ff_head[r * half:(r + 1) * half, :]
             + _dot(gated_ref[r * half:(r + 1) * half, split:], wd_ref[split:, :]) for r in range(2)]
    _segment_ln_tail(x_ref, o_ref, parts, g_ref[...], b_ref[...])


def _ffn_layer(x, w_up, conv_w, conv_b, w_down, ln_g, ln_b):
    s, d = x.shape
    f = w_down.shape[0]
    ts, fc = min(FFN_TS, s), FFN_FC
    n_chunks = f // fc
    assert f % fc == 0 and s % ts == 0 and n_chunks >= 2 and ts % (2 * SUBLANES * SUBLANES) == 0
    wg = w_up[:, :f].astype(BF16)
    wu = w_up[:, f:].astype(BF16)
    pad = (FFN_CONV_W - 1) * SUBLANES
    kern = functools.partial(_ffn_kernel, ts=ts, fc=fc, n_chunks=n_chunks, f=f)
    return pl.pallas_call(
        kern,
        out_shape=jax.ShapeDtypeStruct((s, d), F32),
        grid=(s // ts,),
        in_specs=[_rows(ts, d), _resident((d, f)), _resident((d, f)),
                  _resident((FFN_CONV_W, 2 * f)), _resident((1, 2 * f)),
                  _resident((f, d)), _resident((1, d)), _resident((1, d))],
        out_specs=_rows(ts, d),
        scratch_shapes=[pltpu.VMEM((n_chunks, pad, fc), F32),
                        pltpu.VMEM((n_chunks, pad, fc), F32),
                        pltpu.VMEM((2, pad + ts, fc), F32),
                        pltpu.VMEM((2, pad + ts, fc), F32),
                        pltpu.VMEM((ts, f), BF16)],
        compiler_params=_params(),
        name="ffn_ln",
    )(x, wg, wu, conv_w, conv_b.reshape(1, -1), w_down.astype(BF16),
      ln_g.reshape(1, -1), ln_b.reshape(1, -1))


def _lru_kernel(x_ref, wgate_ref, wrnn_ref, cw_ref, cb_ref, wa_ref, ba_ref, wx_ref,
                bx_ref, lam_ref, wo_ref, g_ref, b_ref, o_ref,
                ubuf, tail, a_buf, b_buf, hcarry, *, ts, w):
    nt = ts // SUBLANES
    pad = (LRU_CONV_W - 1) * SUBLANES

    @pl.when(pl.program_id(0) == 0)
    def _():
        tail[...] = jnp.zeros_like(tail)
        hcarry[...] = jnp.zeros_like(hcarry)

    xp = _to_segments(x_ref[...]).astype(BF16)
    half_gate = _dot(xp, wgate_ref[...])
    rnn_in = _dot(xp, wrnn_ref[...])

    sub = lax.broadcasted_iota(jnp.int32, (SUBLANES, w), 0)
    ubuf[pad:pad + ts, :] = rnn_in
    for j, tile in enumerate(_shifted_tail_tiles(rnn_in, tail, sub, LRU_CONV_W - 1, ts)):
        ubuf[j * SUBLANES:(j + 1) * SUBLANES, :] = tile
    u = cb_ref[...]
    for k in range(LRU_CONV_W):
        u = u + cw_ref[k:k + 1, :] * ubuf[k * SUBLANES:k * SUBLANES + ts, :]

    ub = u.astype(BF16)
    bw = w // LRU_BLOCKS
    th_r = jnp.tanh(jnp.concatenate([_dot(ub[:, g * bw:(g + 1) * bw], wa_ref[g]) for g in range(LRU_BLOCKS)],
                                    axis=1) + 0.5 * ba_ref[...])
    th_i = jnp.tanh(jnp.concatenate([_dot(ub[:, g * bw:(g + 1) * bw], wx_ref[g]) for g in range(LRU_BLOCKS)],
                                    axis=1) + 0.5 * bx_ref[...])

    neg_lam = -lam_ref[...]
    softplus = jnp.maximum(neg_lam, 0.0) + jnp.log(1.0 + jnp.exp(-jnp.abs(neg_lam)))
    k = (-0.5 * LRU_C) * softplus
    a = jnp.exp(k + k * th_r)
    hu = 0.5 * u
    a_buf[...] = a
    b_buf[...] = jnp.sqrt(1.0 - a * a) * (hu + hu * th_i)

    def seg_step(i, carry):
        hl, pr = carry
        r0 = pl.multiple_of(i * SUBLANES, SUBLANES)
        a_t = a_buf[pl.ds(r0, SUBLANES), :]
        hl = a_t * hl + b_buf[pl.ds(r0, SUBLANES), :]
        pr = a_t * pr
        b_buf[pl.ds(r0, SUBLANES), :] = hl
        a_buf[pl.ds(r0, SUBLANES), :] = pr
        return hl, pr

    e, q = lax.fori_loop(0, nt, seg_step, (jnp.zeros((SUBLANES, w), F32), jnp.ones((SUBLANES, w), F32)),
                         unroll=4)
    for d in (1, 2, 4):
        keep = sub >= d
        e = q * jnp.where(keep, pltpu.roll(e, d, 0), 0.0) + e
        q = q * jnp.where(keep, pltpu.roll(q, d, 0), 1.0)
    h0 = hcarry[...]
    seg_end = e + q * h0
    seg_start = jnp.where(sub == 0, h0, pltpu.roll(seg_end, 1, 0))
    hcarry[...] = jnp.broadcast_to(seg_end[SUBLANES - 1:SUBLANES, :], (SUBLANES, w))
    h = b_buf[...].reshape(nt, SUBLANES, w) + a_buf[...].reshape(nt, SUBLANES, w) * seg_start[None]

    y = (_half_gelu2(half_gate) * h.reshape(ts, w)).astype(BF16)
    half = ts // 2
    parts = [_dot(y[r * half:(r + 1) * half, :], wo_ref[...]) for r in range(2)]
    _segment_ln_tail(x_ref, o_ref, parts, g_ref[...], b_ref[...])


def _lru_layer(x, w_in, conv_w, conv_b, w_a, b_a, w_x, b_x, lam, w_out, ln_g, ln_b):
    s, d = x.shape
    w = w_out.shape[0]
    ts = min(LRU_TS, s)
    assert s % ts == 0 and ts % (2 * SUBLANES * SUBLANES) == 0
    bw = w // LRU_BLOCKS
    pad = (LRU_CONV_W - 1) * SUBLANES
    kern = functools.partial(_lru_kernel, ts=ts, w=w)
    return pl.pallas_call(
        kern,
        out_shape=jax.ShapeDtypeStruct((s, d), F32),
        grid=(s // ts,),
        in_specs=[_rows(ts, d), _resident((d, w)), _resident((d, w)),
                  _resident((LRU_CONV_W, w)), _resident((1, w)),
                  _resident((LRU_BLOCKS, bw, bw)), _resident((1, w)),
                  _resident((LRU_BLOCKS, bw, bw)), _resident((1, w)),
                  _resident((1, w)), _resident((w, d)), _resident((1, d)), _resident((1, d))],
        out_specs=_rows(ts, d),
        scratch_shapes=[pltpu.VMEM((pad + ts, w), F32), pltpu.VMEM((pad, w), F32),
                        pltpu.VMEM((ts, w), F32), pltpu.VMEM((ts, w), F32),
                        pltpu.VMEM((SUBLANES, w), F32)],
        compiler_params=_params(),
        name="rglru_ln",
    )(x, (0.5 * w_in[:, :w]).astype(BF16), w_in[:, w:].astype(BF16), conv_w,
      conv_b.reshape(1, -1), (0.5 * w_a).astype(BF16), b_a.reshape(1, -1), (0.5 * w_x).astype(BF16),
      b_x.reshape(1, -1), lam.reshape(1, -1), w_out.astype(BF16), ln_g.reshape(1, -1), ln_b.reshape(1, -1))


HEAD_PAD = LANES
ROPE_LO = QK_NOPE
ROPE_HALF = QK_ROPE // 2


def _rope_group(v, cos, sin_up, sin_dn):
    width = v.shape[1]
    return (v * cos + pltpu.roll(v, ROPE_HALF, 1) * sin_up
            + pltpu.roll(v, width - ROPE_HALF, 1) * sin_dn)


def _mla_proj_kernel(x_ref, pos_ref, invf_ref, wq_ref, wkv_ref, wpe_ref, qg_ref, kvg_ref,
                     wuq_ref, wuk_ref, wuv_ref, vone_ref, q_ref, k_ref, v_ref, *, ts, q_scale):
    xb = x_ref[...].astype(BF16)
    c_q = _dot(xb, wq_ref[...])
    c_kv = _dot(xb, wkv_ref[...])
    k_pe = _dot(xb, wpe_ref[...])

    def rms(v, g):
        return (v * lax.rsqrt(jnp.mean(v * v, axis=-1, keepdims=True) + RMS_EPS) * g).astype(BF16)

    cqn = rms(c_q, qg_ref[...])
    ckvn = rms(c_kv, kvg_ref[...])

    ang = pos_ref[...] * invf_ref[...]
    cos = jnp.cos(ang)
    sin = jnp.sin(ang)
    lane = lax.broadcasted_iota(jnp.int32, (ts, HEAD_PAD), 1)
    sin_up = jnp.where((lane >= ROPE_LO + ROPE_HALF) & (lane < ROPE_LO + QK_ROPE), sin, 0.0)
    sin_dn = jnp.where((lane >= ROPE_LO) & (lane < ROPE_LO + ROPE_HALF), -sin, 0.0)

    k_pe_r = _rope_group(k_pe, cos, sin_up, sin_dn)
    pair = lambda t: jnp.concatenate([t, t], axis=1)
    cos_q, sup_q, sdn_q = pair(cos * q_scale), pair(sin_up * q_scale), pair(sin_dn * q_scale)
    k_pe_r2 = pair(k_pe_r)
    for h in range(0, MLA_HEADS, 2):
        lo, hi = h * HEAD_PAD, (h + 2) * HEAD_PAD
        q2 = _dot(cqn, wuq_ref[:, lo:hi])
        q_ref[:, lo:hi] = _rope_group(q2, cos_q, sup_q, sdn_q).astype(BF16)
        k2 = _dot(ckvn, wuk_ref[:, lo:hi])
        k_ref[:, lo:hi] = (k2 + k_pe_r2).astype(BF16)
    v_ref[...] = (_dot(ckvn, wuv_ref[...]) + vone_ref[...]).astype(BF16)


def _mla_proj(x, pos, w_in, q_norm_g, kv_norm_g, w_uq, w_ukv):
    s, d = x.shape
    ts = min(MLA_TS, s)
    assert s % ts == 0
    hp = MLA_HEADS * HEAD_PAD
    qk = QK_NOPE + QK_ROPE
    pad = HEAD_PAD - qk
    w_q = w_in[:, :Q_LORA].astype(BF16)
    w_kv = w_in[:, Q_LORA:Q_LORA + KV_LORA].astype(BF16)
    w_pe = jnp.pad(w_in[:, Q_LORA + KV_LORA:], ((0, 0), (ROPE_LO, pad))).astype(BF16)
    w_uq_p = jnp.pad(w_uq.reshape(Q_LORA, MLA_HEADS, qk), ((0, 0), (0, 0), (0, pad)))
    w_uq_p = w_uq_p.reshape(Q_LORA, hp).astype(BF16)
    w_ukv3 = w_ukv.reshape(KV_LORA, MLA_HEADS, QK_NOPE + V_HEAD)
    w_uk_p = jnp.pad(w_ukv3[:, :, :QK_NOPE], ((0, 0), (0, 0), (0, HEAD_PAD - QK_NOPE)))
    w_uk_p = w_uk_p.reshape(KV_LORA, hp).astype(BF16)
    w_uv3 = w_ukv3[:, :, QK_NOPE:].reshape(KV_LORA, MLA_HEADS // 2, 2, V_HEAD)
    zero = jnp.zeros_like(w_uv3[:, :, 0])
    w_uv_p = jnp.stack([w_uv3[:, :, 0], zero, zero, w_uv3[:, :, 1]], axis=2).reshape(KV_LORA, hp).astype(BF16)
    pair_ones = jnp.zeros((2 * HEAD_PAD,), F32).at[V_HEAD].set(1.0).at[HEAD_PAD].set(1.0)
    v_ones = jnp.tile(pair_ones, MLA_HEADS // 2).reshape(1, hp)
    inv_freq = ROPE_BASE ** (-jnp.arange(0, QK_ROPE, 2, dtype=F32) / QK_ROPE)
    invf = jnp.concatenate([jnp.zeros((ROPE_LO,), F32), inv_freq, inv_freq, jnp.zeros((pad,), F32)])
    q_scale = qk ** -0.5 * math.log2(math.e)
    kern = functools.partial(_mla_proj_kernel, ts=ts, q_scale=q_scale)
    return pl.pallas_call(
        kern,
        out_shape=(jax.ShapeDtypeStruct((s, hp), BF16), jax.ShapeDtypeStruct((s, hp), BF16),
                   jax.ShapeDtypeStruct((s, hp), BF16)),
        grid=(s // ts,),
        in_specs=[_rows(ts, d), _rows(ts, 1), _resident((1, HEAD_PAD)),
                  _resident((d, Q_LORA)), _resident((d, KV_LORA)), _resident((d, HEAD_PAD)),
                  _resident((1, Q_LORA)), _resident((1, KV_LORA)),
                  _resident((Q_LORA, hp)), _resident((KV_LORA, hp)), _resident((KV_LORA, hp)),
                  _resident((1, hp))],
        out_specs=(_rows(ts, hp), _rows(ts, hp), _rows(ts, hp)),
        compiler_params=_params(),
        name="mla_proj",
    )(x, pos, invf.reshape(1, -1), w_q, w_kv, w_pe, q_norm_g.reshape(1, -1), kv_norm_g.reshape(1, -1),
      w_uq_p, w_uk_p, w_uv_p, v_ones)


HEADS_PER_STEP = 2


def _attn_kernel(q_ref, k_ref, v_ref, o_ref, m_ref, acc_ref, *, tq, tk):
    i = pl.program_id(1)
    m_ref[...] = jnp.full(m_ref.shape, -jnp.inf, F32)
    acc_ref[...] = jnp.zeros_like(acc_ref)
    reps = tk // LANES

    def tile(j, diag_off):
        r0 = pl.multiple_of(j * tk, tk)
        for h in range(HEADS_PER_STEP):
            lo, hi = h * HEAD_PAD, (h + 1) * HEAD_PAD
            s = _dot_nt(q_ref[:, lo:hi], k_ref[pl.ds(r0, tk), lo:hi])
            if diag_off is not None:
                qc = lax.broadcasted_iota(jnp.int32, (tq, tk), 0) // CHUNK
                kc = (lax.broadcasted_iota(jnp.int32, (tq, tk), 1) + diag_off) // CHUNK
                s = jnp.where(kc <= qc, s, -jnp.inf)
            m_old = m_ref[h]
            m_new = jnp.maximum(m_old, jnp.max(s, axis=-1, keepdims=True))
            alpha = jnp.exp2(m_old - m_new)
            p = jnp.exp2(s - jnp.concatenate([m_new] * reps, axis=1))
            acc_ref[h] = alpha * acc_ref[h] + _dot(p.astype(BF16), v_ref[pl.ds(r0, tk), lo:hi])
            m_ref[h] = m_new

    def body(j, carry):
        tile(2 * j, None)
        tile(2 * j + 1, None)
        return carry

    n_diag = tq // tk
    n_full = i * n_diag
    lax.fori_loop(0, n_full // 2, body, 0)

    @pl.when(n_full % 2 == 1)
    def _():
        tile(n_full - 1, None)

    for d in range(n_diag):
        tile(i * n_diag + d, d * tk)

    lane = lax.broadcasted_iota(jnp.int32, (tq, LANES), 1)
    a0 = acc_ref[0]
    a1 = acc_ref[1]
    o0 = a0 / a0[:, V_HEAD:V_HEAD + 1]
    o1 = a1 / a1[:, 0:1]
    o_ref[...] = jnp.where(lane < V_HEAD, o0, o1).astype(BF16)


def _mla_attention(q, k, v):
    s = q.shape[0]
    tq = min(ATT_TQ, s)
    tk = min(ATT_TK, tq)
    assert s % tq == 0 and tq % tk == 0 and tk % CHUNK == 0 and HEADS_PER_STEP * V_HEAD == LANES
    n_pairs = MLA_HEADS // HEADS_PER_STEP
    qw = HEADS_PER_STEP * HEAD_PAD
    kern = functools.partial(_attn_kernel, tq=tq, tk=tk)
    return pl.pallas_call(
        kern,
        out_shape=jax.ShapeDtypeStruct((s, MLA_HEADS * V_HEAD), BF16),
        grid=(n_pairs, s // tq),
        in_specs=[pl.BlockSpec((tq, qw), lambda p, i: (i, p)),
                  pl.BlockSpec((s, qw), lambda p, i: (0, p)),
                  pl.BlockSpec((s, qw), lambda p, i: (0, p))],
        out_specs=pl.BlockSpec((tq, LANES), lambda p, i: (i, p)),
        scratch_shapes=[pltpu.VMEM((HEADS_PER_STEP, tq, LANES), F32),
                        pltpu.VMEM((HEADS_PER_STEP, tq, LANES), F32)],
        compiler_params=_params(2),
        name="mla_attention",
    )(q, k, v)


def _proj_ln_kernel(x_ref, y_ref, w_ref, g_ref, b_ref, o_ref):
    o_ref[...] = _deepnorm_ln(x_ref[...], _dot(y_ref[...], w_ref[...]), g_ref[...], b_ref[...])


def _proj_ln(x, y, w, ln_g, ln_b):
    s, d = x.shape
    kdim = y.shape[1]
    ts = min(PROJ_TS, s)
    assert s % ts == 0
    return pl.pallas_call(
        _proj_ln_kernel,
        out_shape=jax.ShapeDtypeStruct((s, d), F32),
        grid=(s // ts,),
        in_specs=[_rows(ts, d), _rows(ts, kdim), _resident((kdim, d)), _resident((1, d)), _resident((1, d))],
        out_specs=_rows(ts, d),
        compiler_params=_params(),
        name="proj_ln",
    )(x, y, w.astype(BF16), ln_g.reshape(1, -1), ln_b.reshape(1, -1))


def _ret_kernel(x_ref, pos_ref, invf_ref, wq_ref, wk_ref, wv_ref, wg_ref, dmat_ref, xi_ref, zeta_ref,
                gn_ref, wo_ref, g_ref, b_ref, o_ref, state_ref, gated_ref,
                *, tb, dqk, dv, k_scale, block_decay):
    @pl.when(pl.program_id(0) == 0)
    def _():
        state_ref[...] = jnp.zeros_like(state_ref)

    x = x_ref[...]
    xb = x.astype(BF16)
    half = dqk // 2
    ang = pos_ref[...] * invf_ref[...]
    cos = jnp.cos(ang)
    sin = jnp.sin(ang)

    def rope(v):
        v1, v2 = v[:, :half], v[:, half:]
        return jnp.concatenate([v1 * cos - v2 * sin, v1 * sin + v2 * cos], axis=1)

    for h in range(RET_HEADS):
        qh = rope(_dot(xb, wq_ref[:, h * dqk:(h + 1) * dqk]))
        kh = rope(_dot(xb, wk_ref[:, h * dqk:(h + 1) * dqk])) * k_scale
        vh = _dot(xb, wv_ref[:, h * dv:(h + 1) * dv]).astype(BF16)
        gh = _dot(xb, wg_ref[:, h * dv:(h + 1) * dv])
        qb = qh.astype(BF16)
        sc = _dot_nt(qb, kh.astype(BF16)) * dmat_ref[h]
        state = state_ref[h]
        y = _dot(sc.astype(BF16), vh) + _dot(qb, state.astype(BF16)) * xi_ref[h]
        state_ref[h] = state * block_decay[h] + _dot_tn((kh * zeta_ref[h]).astype(BF16), vh)
        mu = jnp.mean(y, axis=-1, keepdims=True)
        dlt = y - mu
        var = jnp.mean(dlt * dlt, axis=-1, keepdims=True)
        yn = dlt * lax.rsqrt(var + LN_EPS) * gn_ref[:, h * dv:(h + 1) * dv]
        gated_ref[:, h * dv:(h + 1) * dv] = (gh * _sigmoid(gh) * yn).astype(BF16)
    o_ref[...] = _deepnorm_ln(x, _dot(gated_ref[...], wo_ref[...]), g_ref[...], b_ref[...])


def _ret_layer(x, pos, w_in, gn_g, w_out, ln_g, ln_b):
    s, d = x.shape
    hv = w_out.shape[0]
    dv = hv // RET_HEADS
    hq = (w_in.shape[1] - 2 * hv) // 2
    dqk = hq // RET_HEADS
    tb = min(RET_TB, s)
    assert s % tb == 0 and tb % CHUNK == 0
    half = dqk // 2
    inv_freq = ROPE_BASE ** (-jnp.arange(0, dqk, 2, dtype=F32) / dqk)
    log_gamma = jnp.log1p(-jnp.exp2(-5.0 - jnp.arange(RET_HEADS, dtype=F32)))
    idx = jnp.arange(tb, dtype=F32)
    rel = jnp.abs(idx[:, None] - idx[None, :])
    ic = jnp.arange(tb) // CHUNK
    visible = ic[None, :] <= ic[:, None]
    dmat = jnp.where(visible[None], jnp.exp(log_gamma[:, None, None] * rel), 0.0)
    xi = jnp.exp(log_gamma[:, None] * (idx + 1.0)[None, :])[:, :, None]
    zeta = jnp.exp(log_gamma[:, None] * (tb - 1.0 - idx)[None, :])[:, :, None]
    block_decay = tuple(float(math.exp(math.log1p(-2.0 ** (-5.0 - h)) * tb)) for h in range(RET_HEADS))
    kern = functools.partial(_ret_kernel, tb=tb, dqk=dqk, dv=dv, k_scale=dqk ** -0.5,
                             block_decay=block_decay)
    return pl.pallas_call(
        kern,
        out_shape=jax.ShapeDtypeStruct((s, d), F32),
        grid=(s // tb,),
        in_specs=[_rows(tb, d), _rows(tb, 1), _resident((1, half)),
                  _resident((d, hq)), _resident((d, hq)), _resident((d, hv)), _resident((d, hv)),
                  _resident((RET_HEADS, tb, tb)), _resident((RET_HEADS, tb, 1)), _resident((RET_HEADS, tb, 1)),
                  _resident((1, hv)), _resident((hv, d)), _resident((1, d)), _resident((1, d))],
        out_specs=_rows(tb, d),
        scratch_shapes=[pltpu.VMEM((RET_HEADS, dqk, dv), F32), pltpu.VMEM((tb, hv), BF16)],
        compiler_params=_params(),
        name="retention_ln",
    )(x, pos, inv_freq.reshape(1, -1),
      w_in[:, :hq].astype(BF16), w_in[:, hq:2 * hq].astype(BF16),
      w_in[:, 2 * hq:2 * hq + hv].astype(BF16), w_in[:, 2 * hq + hv:].astype(BF16),
      dmat, xi, zeta, gn_g.reshape(1, -1), w_out.astype(BF16), ln_g.reshape(1, -1), ln_b.reshape(1, -1))


def kernel(x, positions, ln1_g, ln1_b, ln2_g, ln2_b, ffn_w_up, ffn_conv_w, ffn_conv_b, ffn_w_down,
           lru_w_in, lru_conv_w, lru_conv_b, lru_w_a, lru_b_a, lru_w_x, lru_b_x, lru_lambda, lru_w_out,
           mla_w_in, mla_q_norm, mla_kv_norm, mla_w_uq, mla_w_ukv, mla_w_out,
           ret_w_in, ret_gn_g, ret_w_out):
    bsz, s, d = x.shape
    outs = []
    for bi in range(bsz):
        h = x[bi]
        pos = positions[bi].astype(F32).reshape(s, 1)
        for i in range(DEPTH):
            kind, j = i % N_MIXERS, i // N_MIXERS
            if kind == 0:
                h = _lru_layer(h, lru_w_in[j], lru_conv_w[j], lru_conv_b[j], lru_w_a[j], lru_b_a[j],
                               lru_w_x[j], lru_b_x[j], lru_lambda[j], lru_w_out[j], ln1_g[i], ln1_b[i])
            elif kind == 1:
                q, k, v = _mla_proj(h, pos, mla_w_in[j], mla_q_norm[j], mla_kv_norm[j],
                                    mla_w_uq[j], mla_w_ukv[j])
                o = _mla_attention(q, k, v)
                h = _proj_ln(h, o, mla_w_out[j], ln1_g[i], ln1_b[i])
            else:
                h = _ret_layer(h, pos, ret_w_in[j], ret_gn_g[j], ret_w_out[j], ln1_g[i], ln1_b[i])
            h = _ffn_layer(h, ffn_w_up[i], ffn_conv_w[i], ffn_conv_b[i], ffn_w_down[i], ln2_g[i], ln2_b[i])
        outs.append(h)
    return outs[0][None] if bsz == 1 else jnp.stack(outs, axis=0)
```

```python
import functools
import math

import jax
import jax.numpy as jnp
from jax import lax
from jax.experimental import pallas as pl
from jax.experimental.pallas import tpu as pltpu

F32 = jnp.float32
BF16 = jnp.bfloat16

DEPTH = 4
N_MIXERS = 3
CHUNK = 64
DEEPNORM_ALPHA = (2.0 * DEPTH) ** 0.25
LN_EPS = 1e-5
RMS_EPS = 1e-6
ROPE_BASE = 10000.0

LRU_BLOCKS = 4
LRU_CONV_W = 4
LRU_C = 8.0

MLA_HEADS = 16
QK_NOPE = 64
QK_ROPE = 32
V_HEAD = 64
KV_LORA = 256
Q_LORA = 768

RET_HEADS = 4
FFN_CONV_W = 3

LANES = 128
SUBLANES = 8
VMEM_LIMIT = 56 * 1024 * 1024

FFN_TS = 512
FFN_FC = 256
LRU_TS = 512
MLA_TS = 512
ATT_TQ = 1024
ATT_TK = 1024
RET_TB = 256
PROJ_TS = 1024


def _resident(shape):
    nd = len(shape)
    return pl.BlockSpec(shape, lambda *_: (0,) * nd, pipeline_mode=pl.Buffered(1))


def _rows(ts, width):
    return pl.BlockSpec((ts, width), lambda i: (i, 0))


def _params(n_axes=1):
    return pltpu.CompilerParams(dimension_semantics=("arbitrary",) * n_axes,
                                vmem_limit_bytes=VMEM_LIMIT)


def _dot(a, b):
    return jnp.dot(a, b, preferred_element_type=F32)


def _dot_nt(a, b):
    return lax.dot_general(a, b, (((1,), (1,)), ((), ())), preferred_element_type=F32)


def _dot_tn(a, b):
    return lax.dot_general(a, b, (((0,), (0,)), ((), ())), preferred_element_type=F32)


def _half_gelu2(hx):
    c = math.sqrt(2.0 / math.pi)
    return hx + hx * jnp.tanh(hx * ((8.0 * 0.044715 * c) * (hx * hx) + 2.0 * c))


def _sigmoid(x):
    return 1.0 / (1.0 + jnp.exp(-x))


def _deepnorm_ln(x, mix, g, b):
    y = DEEPNORM_ALPHA * x + mix
    mu = jnp.mean(y, axis=-1, keepdims=True)
    d = y - mu
    var = jnp.mean(d * d, axis=-1, keepdims=True)
    return d * lax.rsqrt(var + LN_EPS) * g + b


def _to_segments(x):
    rows, d = x.shape
    return jnp.swapaxes(x.reshape(SUBLANES, rows // SUBLANES, d), 0, 1).reshape(rows, d)


def _segment_ln_tail(x_ref, o_ref, parts, g, b):
    ts, d = x_ref.shape
    seg_len = ts // SUBLANES
    run = seg_len // len(parts)
    for r, part in enumerate(parts):
        nat = jnp.swapaxes(part.reshape(run, SUBLANES, d), 0, 1).reshape(SUBLANES * run, d)
        rows = [slice(s * seg_len + r * run, s * seg_len + (r + 1) * run) for s in range(SUBLANES)]
        out = _deepnorm_ln(jnp.concatenate([x_ref[rs, :] for rs in rows], axis=0), nat, g, b)
        for s, rs in enumerate(rows):
            o_ref[rs, :] = out[s * run:(s + 1) * run, :]


def _shifted_tail_tiles(h, tail, sub, n_back, ts):
    tiles = []
    for j in range(n_back, 0, -1):
        cur = h[ts - j * SUBLANES:ts - (j - 1) * SUBLANES, :]
        prev = tail[(j - 1) * SUBLANES:j * SUBLANES, :]
        tiles.append(jnp.where(sub == 0, pltpu.roll(prev, 1, 0), pltpu.roll(cur, 1, 0)))
        tail[(j - 1) * SUBLANES:j * SUBLANES, :] = cur
    return tiles


def _ffn_kernel(x_ref, wg_ref, wu_ref, cw_ref, cb_ref, wd_ref, g_ref, b_ref, o_ref,
                tail_g, tail_u, work_g, work_u, gated_ref, xb_ref, *, ts, fc, n_chunks, f):
    pad = (FFN_CONV_W - 1) * SUBLANES

    @pl.when(pl.program_id(0) == 0)
    def _():
        tail_g[...] = jnp.zeros_like(tail_g)
        tail_u[...] = jnp.zeros_like(tail_u)

    xb_ref[...] = _to_segments(x_ref[...]).astype(BF16)
    sub = lax.broadcasted_iota(jnp.int32, (SUBLANES, fc), 0)

    def up_proj(c):
        lo, hi = c * fc, (c + 1) * fc
        for w_ref, tail, work in ((wg_ref, tail_g, work_g), (wu_ref, tail_u, work_u)):
            h = _dot(xb_ref[...], w_ref[:, lo:hi])
            work[c % 2, pad:pad + ts, :] = h
            for j, tile in enumerate(_shifted_tail_tiles(h, tail.at[c], sub, FFN_CONV_W - 1, ts)):
                work[c % 2, j * SUBLANES:(j + 1) * SUBLANES, :] = tile

    def conv_gate(c):
        for part in range(fc // LANES):
            p0, p1 = part * LANES, (part + 1) * LANES
            lo, hi = c * fc + p0, c * fc + p1
            conv = []
            for work, off, scale in ((work_g, 0, 0.5), (work_u, f, 1.0)):
                y = cb_ref[:, off + lo:off + hi] * scale
                for k in range(FFN_CONV_W):
                    y = y + (cw_ref[k:k + 1, off + lo:off + hi] * scale) * work[c % 2, k * SUBLANES:k * SUBLANES + ts, p0:p1]
                conv.append(y)
            gated_ref[:, lo:hi] = (_half_gelu2(conv[0]) * conv[1]).astype(BF16)

    up_proj(0)
    split = (n_chunks // 2 + 1) * fc
    for c in range(n_chunks):
        if c + 1 < n_chunks:
            up_proj(c + 1)
        conv_gate(c)
        if (c + 1) * fc == split:
            ff_head = _dot(gated_ref[:, :split], wd_ref[:split, :])
    half = ts // 2
    parts = [ff_head[r * half:(r + 1) * half, :]
             + _dot(gated_ref[r * half:(r + 1) * half, split:], wd_ref[split:, :]) for r in range(2)]
    _segment_ln_tail(x_ref, o_ref, parts, g_ref[...], b_ref[...])


def _ffn_layer(x, w_up, conv_w, conv_b, w_down, ln_g, ln_b):
    s, d = x.shape
    f = w_down.shape[0]
    ts, fc = min(FFN_TS, s), FFN_FC
    n_chunks = f // fc
    assert f % fc == 0 and s % ts == 0 and n_chunks >= 2 and ts % (2 * SUBLANES * SUBLANES) == 0
    wg = w_up[:, :f].astype(BF16)
    wu = w_up[:, f:].astype(BF16)
    pad = (FFN_CONV_W - 1) * SUBLANES
    kern = functools.partial(_ffn_kernel, ts=ts, fc=fc, n_chunks=n_chunks, f=f)
    return pl.pallas_call(
        kern,
        out_shape=jax.ShapeDtypeStruct((s, d), F32),
        grid=(s // ts,),
        in_specs=[_rows(ts, d), _resident((d, f)), _resident((d, f)),
                  _resident((FFN_CONV_W, 2 * f)), _resident((1, 2 * f)),
                  _resident((f, d)), _resident((1, d)), _resident((1, d))],
        out_specs=_rows(ts, d),
        scratch_shapes=[pltpu.VMEM((n_chunks, pad, fc), F32),
                        pltpu.VMEM((n_chunks, pad, fc), F32),
                        pltpu.VMEM((2, pad + ts, fc), F32),
                        pltpu.VMEM((2, pad + ts, fc), F32),
                        pltpu.VMEM((ts, f), BF16), pltpu.VMEM((ts, d), BF16)],
        compiler_params=_params(),
        name="ffn_ln",
    )(x, wg, wu, conv_w, conv_b.reshape(1, -1), w_down.astype(BF16),
      ln_g.reshape(1, -1), ln_b.reshape(1, -1))


def _lru_kernel(x_ref, wgate_ref, wrnn_ref, cw_ref, cb_ref, wa_ref, ba_ref, wx_ref,
                bx_ref, lam_ref, wo_ref, g_ref, b_ref, o_ref,
                ubuf, tail, a_buf, b_buf, hcarry, *, ts, w):
    nt = ts // SUBLANES
    pad = (LRU_CONV_W - 1) * SUBLANES

    @pl.when(pl.program_id(0) == 0)
    def _():
        tail[...] = jnp.zeros_like(tail)
        hcarry[...] = jnp.zeros_like(hcarry)

    xp = _to_segments(x_ref[...]).astype(BF16)
    half_gate = _dot(xp, wgate_ref[...])
    rnn_in = _dot(xp, wrnn_ref[...])

    sub = lax.broadcasted_iota(jnp.int32, (SUBLANES, w), 0)
    ubuf[pad:pad + ts, :] = rnn_in
    for j, tile in enumerate(_shifted_tail_tiles(rnn_in, tail, sub, LRU_CONV_W - 1, ts)):
        ubuf[j * SUBLANES:(j + 1) * SUBLANES, :] = tile
    u = cb_ref[...]
    for k in range(LRU_CONV_W):
        u = u + cw_ref[k:k + 1, :] * ubuf[k * SUBLANES:k * SUBLANES + ts, :]

    ub = u.astype(BF16)
    bw = w // LRU_BLOCKS
    th_r = jnp.tanh(jnp.concatenate([_dot(ub[:, g * bw:(g + 1) * bw], wa_ref[g]) for g in range(LRU_BLOCKS)],
                                    axis=1) + 0.5 * ba_ref[...])
    th_i = jnp.tanh(jnp.concatenate([_dot(ub[:, g * bw:(g + 1) * bw], wx_ref[g]) for g in range(LRU_BLOCKS)],
                                    axis=1) + 0.5 * bx_ref[...])

    neg_lam = -lam_ref[...]
    softplus = jnp.maximum(neg_lam, 0.0) + jnp.log(1.0 + jnp.exp(-jnp.abs(neg_lam)))
    k = (-0.5 * LRU_C) * softplus
    a = jnp.exp(k + k * th_r)
    hu = 0.5 * u
    a_buf[...] = a
    b_buf[...] = jnp.sqrt(1.0 - a * a) * (hu + hu * th_i)

    def seg_step(i, carry):
        hl, pr = carry
        r0 = pl.multiple_of(i * SUBLANES, SUBLANES)
        a_t = a_buf[pl.ds(r0, SUBLANES), :]
        hl = a_t * hl + b_buf[pl.ds(r0, SUBLANES), :]
        pr = a_t * pr
        b_buf[pl.ds(r0, SUBLANES), :] = hl
        a_buf[pl.ds(r0, SUBLANES), :] = pr
        return hl, pr

    e, q = lax.fori_loop(0, nt, seg_step, (jnp.zeros((SUBLANES, w), F32), jnp.ones((SUBLANES, w), F32)),
                         unroll=4)
    for d in (1, 2, 4):
        keep = sub >= d
        e = q * jnp.where(keep, pltpu.roll(e, d, 0), 0.0) + e
        q = q * jnp.where(keep, pltpu.roll(q, d, 0), 1.0)
    h0 = hcarry[...]
    seg_end = e + q * h0
    seg_start = jnp.where(sub == 0, h0, pltpu.roll(seg_end, 1, 0))
    hcarry[...] = jnp.broadcast_to(seg_end[SUBLANES - 1:SUBLANES, :], (SUBLANES, w))
    h = b_buf[...].reshape(nt, SUBLANES, w) + a_buf[...].reshape(nt, SUBLANES, w) * seg_start[None]

    y = (_half_gelu2(half_gate) * h.reshape(ts, w)).astype(BF16)
    half = ts // 2
    parts = [_dot(y[r * half:(r + 1) * half, :], wo_ref[...]) for r in range(2)]
    _segment_ln_tail(x_ref, o_ref, parts, g_ref[...], b_ref[...])


def _lru_layer(x, w_in, conv_w, conv_b, w_a, b_a, w_x, b_x, lam, w_out, ln_g, ln_b):
    s, d = x.shape
    w = w_out.shape[0]
    ts = min(LRU_TS, s)
    assert s % ts == 0 and ts % (2 * SUBLANES * SUBLANES) == 0
    bw = w // LRU_BLOCKS
    pad = (LRU_CONV_W - 1) * SUBLANES
    kern = functools.partial(_lru_kernel, ts=ts, w=w)
    return pl.pallas_call(
        kern,
        out_shape=jax.ShapeDtypeStruct((s, d), F32),
        grid=(s // ts,),
        in_specs=[_rows(ts, d), _resident((d, w)), _resident((d, w)),
                  _resident((LRU_CONV_W, w)), _resident((1, w)),
                  _resident((LRU_BLOCKS, bw, bw)), _resident((1, w)),
                  _resident((LRU_BLOCKS, bw, bw)), _resident((1, w)),
                  _resident((1, w)), _resident((w, d)), _resident((1, d)), _resident((1, d))],
        out_specs=_rows(ts, d),
        scratch_shapes=[pltpu.VMEM((pad + ts, w), F32), pltpu.VMEM((pad, w), F32),
                        pltpu.VMEM((ts, w), F32), pltpu.VMEM((ts, w), F32),
                        pltpu.VMEM((SUBLANES, w), F32)],
        compiler_params=_params(),
        name="rglru_ln",
    )(x, (0.5 * w_in[:, :w]).astype(BF16), w_in[:, w:].astype(BF16), conv_w,
      conv_b.reshape(1, -1), (0.5 * w_a).astype(BF16), b_a.reshape(1, -1), (0.5 * w_x).astype(BF16),
      b_x.reshape(1, -1), lam.reshape(1, -1), w_out.astype(BF16), ln_g.reshape(1, -1), ln_b.reshape(1, -1))


HEAD_PAD = LANES
ROPE_LO = QK_NOPE
ROPE_HALF = QK_ROPE // 2


def _rope_group(v, cos, sin_up, sin_dn):
    width = v.shape[1]
    return (v * cos + pltpu.roll(v, ROPE_HALF, 1) * sin_up
            + pltpu.roll(v, width - ROPE_HALF, 1) * sin_dn)


def _mla_proj_kernel(x_ref, pos_ref, invf_ref, wq_ref, wkv_ref, wpe_ref, qg_ref, kvg_ref,
                     wuq_ref, wuk_ref, wuv_ref, vone_ref, q_ref, k_ref, v_ref, *, ts, q_scale):
    xb = x_ref[...].astype(BF16)
    c_q = _dot(xb, wq_ref[...])
    c_kv = _dot(xb, wkv_ref[...])
    k_pe = _dot(xb, wpe_ref[...])

    def rms(v, g):
        return (v * lax.rsqrt(jnp.mean(v * v, axis=-1, keepdims=True) + RMS_EPS) * g).astype(BF16)

    cqn = rms(c_q, qg_ref[...])
    ckvn = rms(c_kv, kvg_ref[...])

    ang = pos_ref[...] * invf_ref[...]
    cos = jnp.cos(ang)
    sin = jnp.sin(ang)
    lane = lax.broadcasted_iota(jnp.int32, (ts, HEAD_PAD), 1)
    sin_up = jnp.where((lane >= ROPE_LO + ROPE_HALF) & (lane < ROPE_LO + QK_ROPE), sin, 0.0)
    sin_dn = jnp.where((lane >= ROPE_LO) & (lane < ROPE_LO + ROPE_HALF), -sin, 0.0)

    k_pe_r = _rope_group(k_pe, cos, sin_up, sin_dn)
    pair = lambda t: jnp.concatenate([t, t], axis=1)
    cos_q, sup_q, sdn_q = pair(cos * q_scale), pair(sin_up * q_scale), pair(sin_dn * q_scale)
    k_pe_r2 = pair(k_pe_r)
    for h in range(0, MLA_HEADS, 2):
        lo, hi = h * HEAD_PAD, (h + 2) * HEAD_PAD
        q2 = _dot(cqn, wuq_ref[:, lo:hi])
        q_ref[:, lo:hi] = _rope_group(q2, cos_q, sup_q, sdn_q).astype(BF16)
        k2 = _dot(ckvn, wuk_ref[:, lo:hi])
        k_ref[:, lo:hi] = (k2 + k_pe_r2).astype(BF16)
    v_ref[...] = (_dot(ckvn, wuv_ref[...]) + vone_ref[...]).astype(BF16)


def _mla_proj(x, pos, w_in, q_norm_g, kv_norm_g, w_uq, w_ukv):
    s, d = x.shape
    ts = min(MLA_TS, s)
    assert s % ts == 0
    hp = MLA_HEADS * HEAD_PAD
    qk = QK_NOPE + QK_ROPE
    pad = HEAD_PAD - qk
    w_q = w_in[:, :Q_LORA].astype(BF16)
    w_kv = w_in[:, Q_LORA:Q_LORA + KV_LORA].astype(BF16)
    w_pe = jnp.pad(w_in[:, Q_LORA + KV_LORA:], ((0, 0), (ROPE_LO, pad))).astype(BF16)
    w_uq_p = jnp.pad(w_uq.reshape(Q_LORA, MLA_HEADS, qk), ((0, 0), (0, 0), (0, pad)))
    w_uq_p = w_uq_p.reshape(Q_LORA, hp).astype(BF16)
    w_ukv3 = w_ukv.reshape(KV_LORA, MLA_HEADS, QK_NOPE + V_HEAD)
    w_uk_p = jnp.pad(w_ukv3[:, :, :QK_NOPE], ((0, 0), (0, 0), (0, HEAD_PAD - QK_NOPE)))
    w_uk_p = w_uk_p.reshape(KV_LORA, hp).astype(BF16)
    w_uv3 = w_ukv3[:, :, QK_NOPE:].reshape(KV_LORA, MLA_HEADS // 2, 2, V_HEAD)
    zero = jnp.zeros_like(w_uv3[:, :, 0])
    w_uv_p = jnp.stack([w_uv3[:, :, 0], zero, zero, w_uv3[:, :, 1]], axis=2).reshape(KV_LORA, hp).astype(BF16)
    pair_ones = jnp.zeros((2 * HEAD_PAD,), F32).at[V_HEAD].set(1.0).at[HEAD_PAD].set(1.0)
    v_ones = jnp.tile(pair_ones, MLA_HEADS // 2).reshape(1, hp)
    inv_freq = ROPE_BASE ** (-jnp.arange(0, QK_ROPE, 2, dtype=F32) / QK_ROPE)
    invf = jnp.concatenate([jnp.zeros((ROPE_LO,), F32), inv_freq, inv_freq, jnp.zeros((pad,), F32)])
    q_scale = qk ** -0.5 * math.log2(math.e)
    kern = functools.partial(_mla_proj_kernel, ts=ts, q_scale=q_scale)
    return pl.pallas_call(
        kern,
        out_shape=(jax.ShapeDtypeStruct((s, hp), BF16), jax.ShapeDtypeStruct((s, hp), BF16),
                   jax.ShapeDtypeStruct((s, hp), BF16)),
        grid=(s // ts,),
        in_specs=[_rows(ts, d), _rows(ts, 1), _resident((1, HEAD_PAD)),
                  _resident((d, Q_LORA)), _resident((d, KV_LORA)), _resident((d, HEAD_PAD)),
                  _resident((1, Q_LORA)), _resident((1, KV_LORA)),
                  _resident((Q_LORA, hp)), _resident((KV_LORA, hp)), _resident((KV_LORA, hp)),
                  _resident((1, hp))],
        out_specs=(_rows(ts, hp), _rows(ts, hp), _rows(ts, hp)),
        compiler_params=_params(),
        name="mla_proj",
    )(x, pos, invf.reshape(1, -1), w_q, w_kv, w_pe, q_norm_g.reshape(1, -1), kv_norm_g.reshape(1, -1),
      w_uq_p, w_uk_p, w_uv_p, v_ones)


HEADS_PER_STEP = 2


def _attn_kernel(q_ref, k_ref, v_ref, o_ref, m_ref, acc_ref, *, tq, tk):
    i = pl.program_id(1)
    m_ref[...] = jnp.full(m_ref.shape, -jnp.inf, F32)
    acc_ref[...] = jnp.zeros_like(acc_ref)
    reps = tk // LANES

    def tile(j, diag_off):
        r0 = pl.multiple_of(j * tk, tk)
        for h in range(HEADS_PER_STEP):
            lo, hi = h * HEAD_PAD, (h + 1) * HEAD_PAD
            s = _dot_nt(q_ref[:, lo:hi], k_ref[pl.ds(r0, tk), lo:hi])
            if diag_off is not None:
                qc = lax.broadcasted_iota(jnp.int32, (tq, tk), 0) // CHUNK
                kc = (lax.broadcasted_iota(jnp.int32, (tq, tk), 1) + diag_off) // CHUNK
                s = jnp.where(kc <= qc, s, -jnp.inf)
            m_old = m_ref[h]
            m_new = jnp.maximum(m_old, jnp.max(s, axis=-1, keepdims=True))
            alpha = jnp.exp2(m_old - m_new)
            p = jnp.exp2(s - jnp.concatenate([m_new] * reps, axis=1))
            acc_ref[h] = alpha * acc_ref[h] + _dot(p.astype(BF16), v_ref[pl.ds(r0, tk), lo:hi])
            m_ref[h] = m_new

    def body(j, carry):
        tile(2 * j, None)
        tile(2 * j + 1, None)
        return carry

    n_diag = tq // tk
    n_full = i * n_diag
    lax.fori_loop(0, n_full // 2, body, 0)

    @pl.when(n_full % 2 == 1)
    def _():
        tile(n_full - 1, None)

    for d in range(n_diag):
        tile(i * n_diag + d, d * tk)

    lane = lax.broadcasted_iota(jnp.int32, (tq, LANES), 1)
    a0 = acc_ref[0]
    a1 = acc_ref[1]
    o0 = a0 / a0[:, V_HEAD:V_HEAD + 1]
    o1 = a1 / a1[:, 0:1]
    o_ref[...] = jnp.where(lane < V_HEAD, o0, o1).astype(BF16)


def _mla_attention(q, k, v):
    s = q.shape[0]
    tq = min(ATT_TQ, s)
    tk = min(ATT_TK, tq)
    assert s % tq == 0 and tq % tk == 0 and tk % CHUNK == 0 and HEADS_PER_STEP * V_HEAD == LANES
    n_pairs = MLA_HEADS // HEADS_PER_STEP
    qw = HEADS_PER_STEP * HEAD_PAD
    kern = functools.partial(_attn_kernel, tq=tq, tk=tk)
    return pl.pallas_call(
        kern,
        out_shape=jax.ShapeDtypeStruct((s, MLA_HEADS * V_HEAD), BF16),
        grid=(n_pairs, s // tq),
        in_specs=[pl.BlockSpec((tq, qw), lambda p, i: (i, p)),
                  pl.BlockSpec((s, qw), lambda p, i: (0, p)),
                  pl.BlockSpec((s, qw), lambda p, i: (0, p))],
        out_specs=pl.BlockSpec((tq, LANES), lambda p, i: (i, p)),
        scratch_shapes=[pltpu.VMEM((HEADS_PER_STEP, tq, LANES), F32),
                        pltpu.VMEM((HEADS_PER_STEP, tq, LANES), F32)],
        compiler_params=_params(2),
        name="mla_attention",
    )(q, k, v)


def _proj_ln_kernel(x_ref, y_ref, w_ref, g_ref, b_ref, o_ref):
    o_ref[...] = _deepnorm_ln(x_ref[...], _dot(y_ref[...], w_ref[...]), g_ref[...], b_ref[...])


def _proj_ln(x, y, w, ln_g, ln_b):
    s, d = x.shape
    kdim = y.shape[1]
    ts = min(PROJ_TS, s)
    assert s % ts == 0
    return pl.pallas_call(
        _proj_ln_kernel,
        out_shape=jax.ShapeDtypeStruct((s, d), F32),
        grid=(s // ts,),
        in_specs=[_rows(ts, d), _rows(ts, kdim), _resident((kdim, d)), _resident((1, d)), _resident((1, d))],
        out_specs=_rows(ts, d),
        compiler_params=_params(),
        name="proj_ln",
    )(x, y, w.astype(BF16), ln_g.reshape(1, -1), ln_b.reshape(1, -1))


def _ret_kernel(x_ref, pos_ref, invf_ref, wq_ref, wk_ref, wv_ref, wg_ref, dmat_ref, xi_ref, zeta_ref,
                gn_ref, wo_ref, g_ref, b_ref, o_ref, state_ref, gated_ref,
                *, tb, dqk, dv, k_scale, block_decay):
    @pl.when(pl.program_id(0) == 0)
    def _():
        state_ref[...] = jnp.zeros_like(state_ref)

    x = x_ref[...]
    xb = x.astype(BF16)
    half = dqk // 2
    ang = pos_ref[...] * invf_ref[...]
    cos = jnp.cos(ang)
    sin = jnp.sin(ang)

    def rope(v):
        v1, v2 = v[:, :half], v[:, half:]
        return jnp.concatenate([v1 * cos - v2 * sin, v1 * sin + v2 * cos], axis=1)

    for h in range(RET_HEADS):
        qh = rope(_dot(xb, wq_ref[:, h * dqk:(h + 1) * dqk]))
        kh = rope(_dot(xb, wk_ref[:, h * dqk:(h + 1) * dqk])) * k_scale
        vh = _dot(xb, wv_ref[:, h * dv:(h + 1) * dv]).astype(BF16)
        gh = _dot(xb, wg_ref[:, h * dv:(h + 1) * dv])
        qb = qh.astype(BF16)
        sc = _dot_nt(qb, kh.astype(BF16)) * dmat_ref[h]
        state = state_ref[h]
        y = _dot(sc.astype(BF16), vh) + _dot(qb, state.astype(BF16)) * xi_ref[h]
        state_ref[h] = state * block_decay[h] + _dot_tn((kh * zeta_ref[h]).astype(BF16), vh)
        mu = jnp.mean(y, axis=-1, keepdims=True)
        dlt = y - mu
        var = jnp.mean(dlt * dlt, axis=-1, keepdims=True)
        yn = dlt * lax.rsqrt(var + LN_EPS) * gn_ref[:, h * dv:(h + 1) * dv]
        gated_ref[:, h * dv:(h + 1) * dv] = (gh * _sigmoid(gh) * yn).astype(BF16)
    o_ref[...] = _deepnorm_ln(x, _dot(gated_ref[...], wo_ref[...]), g_ref[...], b_ref[...])


def _ret_layer(x, pos, w_in, gn_g, w_out, ln_g, ln_b):
    s, d = x.shape
    hv = w_out.shape[0]
    dv = hv // RET_HEADS
    hq = (w_in.shape[1] - 2 * hv) // 2
    dqk = hq // RET_HEADS
    tb = min(RET_TB, s)
    assert s % tb == 0 and tb % CHUNK == 0
    half = dqk // 2
    inv_freq = ROPE_BASE ** (-jnp.arange(0, dqk, 2, dtype=F32) / dqk)
    log_gamma = jnp.log1p(-jnp.exp2(-5.0 - jnp.arange(RET_HEADS, dtype=F32)))
    idx = jnp.arange(tb, dtype=F32)
    rel = jnp.abs(idx[:, None] - idx[None, :])
    ic = jnp.arange(tb) // CHUNK
    visible = ic[None, :] <= ic[:, None]
    dmat = jnp.where(visible[None], jnp.exp(log_gamma[:, None, None] * rel), 0.0)
    xi = jnp.exp(log_gamma[:, None] * (idx + 1.0)[None, :])[:, :, None]
    zeta = jnp.exp(log_gamma[:, None] * (tb - 1.0 - idx)[None, :])[:, :, None]
    block_decay = tuple(float(math.exp(math.log1p(-2.0 ** (-5.0 - h)) * tb)) for h in range(RET_HEADS))
    kern = functools.partial(_ret_kernel, tb=tb, dqk=dqk, dv=dv, k_scale=dqk ** -0.5,
                             block_decay=block_decay)
    return pl.pallas_call(
        kern,
        out_shape=jax.ShapeDtypeStruct((s, d), F32),
        grid=(s // tb,),
        in_specs=[_rows(tb, d), _rows(tb, 1), _resident((1, half)),
                  _resident((d, hq)), _resident((d, hq)), _resident((d, hv)), _resident((d, hv)),
                  _resident((RET_HEADS, tb, tb)), _resident((RET_HEADS, tb, 1)), _resident((RET_HEADS, tb, 1)),
                  _resident((1, hv)), _resident((hv, d)), _resident((1, d)), _resident((1, d))],
        out_specs=_rows(tb, d),
        scratch_shapes=[pltpu.VMEM((RET_HEADS, dqk, dv), F32), pltpu.VMEM((tb, hv), BF16)],
        compiler_params=_params(),
        name="retention_ln",
    )(x, pos, inv_freq.reshape(1, -1),
      w_in[:, :hq].astype(BF16), w_in[:, hq:2 * hq].astype(BF16),
      w_in[:, 2 * hq:2 * hq + hv].astype(BF16), w_in[:, 2 * hq + hv:].astype(BF16),
      dmat, xi, zeta, gn_g.reshape(1, -1), w_out.astype(BF16), ln_g.reshape(1, -1), ln_b.reshape(1, -1))


def kernel(x, positions, ln1_g, ln1_b, ln2_g, ln2_b, ffn_w_up, ffn_conv_w, ffn_conv_b, ffn_w_down,
           lru_w_in, lru_conv_w, lru_conv_b, lru_w_a, lru_b_a, lru_w_x, lru_b_x, lru_lambda, lru_w_out,
           mla_w_in, mla_q_norm, mla_kv_norm, mla_w_uq, mla_w_ukv, mla_w_out,
           ret_w_in, ret_gn_g, ret_w_out):
    bsz, s, d = x.shape
    outs = []
    for bi in range(bsz):
        h = x[bi]
        pos = positions[bi].astype(F32).reshape(s, 1)
        for i in range(DEPTH):
            kind, j = i % N_MIXERS, i // N_MIXERS
            if kind == 0:
                h = _lru_layer(h, lru_w_in[j], lru_conv_w[j], lru_conv_b[j], lru_w_a[j], lru_b_a[j],
                               lru_w_x[j], lru_b_x[j], lru_lambda[j], lru_w_out[j], ln1_g[i], ln1_b[i])
            elif kind == 1:
                q, k, v = _mla_proj(h, pos, mla_w_in[j], mla_q_norm[j], mla_kv_norm[j],
                                    mla_w_uq[j], mla_w_ukv[j])
                o = _mla_attention(q, k, v)
                h = _proj_ln(h, o, mla_w_out[j], ln1_g[i], ln1_b[i])
            else:
                h = _ret_layer(h, pos, ret_w_in[j], ret_gn_g[j], ret_w_out[j], ln1_g[i], ln1_b[i])
            h = _ffn_layer(h, ffn_w_up[i], ffn_conv_w[i], ffn_conv_b[i], ffn_w_down[i], ln2_g[i], ln2_b[i])
        outs.append(h)
    return outs[0][None] if bsz == 1 else jnp.stack(outs, axis=0)
```
